```python
import math
import jax, jax.numpy as jnp
from jax import lax
import numpy as np

D_MODEL = 1024
BATCH = 2
SEQ = 8192
DEPTH = 4

GRID_W = 64
ROPE_THETA = 10000.0
N_MIXERS = 3
EPS = 1e-6

SSM_EXPAND = 2
SSM_D_INNER = SSM_EXPAND * D_MODEL
SSM_HEAD_DIM = 64
SSM_HEADS = SSM_D_INNER // SSM_HEAD_DIM
SSM_GROUPS = 8
SSM_STATE = 128
SSM_CONV = 5
SSM_CHUNK = 128
SSM_CONV_DIM = SSM_D_INNER + 2 * SSM_GROUPS * SSM_STATE
SSM_IN_DIM = SSM_D_INNER + SSM_CONV_DIM + 2 * SSM_HEADS

ATT_HEAD_DIM = 128
ATT_Q_HEADS = D_MODEL // ATT_HEAD_DIM
ATT_KV_HEADS = 2
ATT_BLOCK = 128
ATT_IN_DIM = (ATT_Q_HEADS + 2 * ATT_KV_HEADS) * ATT_HEAD_DIM

RET_HEADS = 4
RET_QK_DIM = D_MODEL // RET_HEADS
RET_V_DIM = 2 * RET_QK_DIM
RET_CHUNK = 128
RET_IN_DIM = 2 * D_MODEL + 2 * (2 * D_MODEL)

D_FF = 4 * D_MODEL

N_A = (DEPTH + 2) // 3
N_B = (DEPTH + 1) // 3
N_C = DEPTH // 3

kernel_name = "hybrid_ssd_gqa_retention_encoder"


def rmsnorm(x, w):
    xf = x.astype(jnp.float32)
    y = xf * lax.rsqrt(jnp.mean(xf * xf, axis=-1, keepdims=True) + EPS)
    return (y * w.astype(jnp.float32)).astype(x.dtype)


def rmsnorm_plain(x):
    xf = x.astype(jnp.float32)
    return (xf * lax.rsqrt(jnp.mean(xf * xf, axis=-1, keepdims=True) + EPS)).astype(x.dtype)


def flip_seq(t):
    return jnp.flip(t, axis=1)


def axial_rope_tables(S, dim):
    rows = S // GRID_W
    row = jnp.repeat(jnp.arange(rows, dtype=jnp.float32), GRID_W)
    col = jnp.tile(jnp.arange(GRID_W, dtype=jnp.float32), rows)
    half = dim // 2
    nf = half // 2
    inv = ROPE_THETA ** (-(jnp.arange(nf, dtype=jnp.float32) * 2.0) / half)
    ang = jnp.stack([row[:, None] * inv, col[:, None] * inv], axis=1)
    return jnp.cos(ang), jnp.sin(ang)


def apply_axial_rope(x, cos, sin):
    b, S, H, dim = x.shape
    nf = dim // 4
    xr = x.astype(jnp.float32).reshape(b, S, H, 2, 2, nf)
    x1 = xr[..., 0, :]
    x2 = xr[..., 1, :]
    c = cos[:, None]
    s = sin[:, None]
    out = jnp.stack([x1 * c - x2 * s, x2 * c + x1 * s], axis=-2)
    return out.reshape(b, S, H, dim).astype(x.dtype)


def depthwise_conv_centred(x, w, bias):
    pad = w.shape[0] // 2
    y = lax.conv_general_dilated(
        x, w[:, None, :].astype(x.dtype), window_strides=(1,), padding=[(pad, pad)],
        dimension_numbers=("NWC", "WIO", "NWC"), feature_group_count=x.shape[-1])
    return y + bias.astype(x.dtype)


def ssd_scan(x, dt, A, Bm, Cm, strict):
    b, S, H, P = x.shape
    G, N = Bm.shape[2], Bm.shape[3]
    hpg = H // G
    L = SSM_CHUNK
    c = S // L
    xc = x.astype(jnp.float32).reshape(b, c, L, G, hpg, P)
    dtc = dt.reshape(b, c, L, G, hpg)
    Bc = Bm.astype(jnp.float32).reshape(b, c, L, G, N)
    Cc = Cm.astype(jnp.float32).reshape(b, c, L, G, N)
    acum = jnp.cumsum(dtc * A.reshape(G, hpg), axis=2)
    mask = jnp.tril(jnp.ones((L, L), dtype=bool), k=-1 if strict else 0)[:, :, None, None]
    seg = acum[:, :, :, None] - acum[:, :, None, :]
    decay = jnp.where(mask, jnp.exp(jnp.where(mask, seg, 0.0)), 0.0)
    cb = jnp.einsum("bclgn,bcsgn->bclsg", Cc, Bc)
    w = cb[..., None] * decay * dtc[:, :, None]
    y_diag = jnp.einsum("bclsgh,bcsghp->bclghp", w, xc)
    decay_to_end = jnp.exp(acum[:, :, -1:] - acum)
    states = jnp.einsum("bclgn,bclgh,bclghp->bcghpn", Bc, decay_to_end * dtc, xc)
    chunk_decay = jnp.exp(acum[:, :, -1])

    def step(h, inp):
        st, dec = inp
        return h * dec[..., None, None] + st, h

    h0 = jnp.zeros((b, G, hpg, P, N), jnp.float32)
    _, prev = lax.scan(step, h0, (jnp.swapaxes(states, 0, 1), jnp.swapaxes(chunk_decay, 0, 1)))
    prev = jnp.swapaxes(prev, 0, 1)
    y_off = jnp.einsum("bclgn,bcghpn,bclgh->bclghp", Cc, prev, jnp.exp(acum))
    return (y_diag + y_off).reshape(b, S, H, P).astype(x.dtype)


def mamba2_mixer(h, w_in, conv_w, conv_b, dt_bias, a_log, d_skip, norm_w, w_out):
    b, S, _ = h.shape
    proj = h @ w_in
    z = proj[..., :SSM_D_INNER]
    xbc = proj[..., SSM_D_INNER:SSM_D_INNER + SSM_CONV_DIM]
    dt_raw = proj[..., SSM_D_INNER + SSM_CONV_DIM:]
    xbc = jax.nn.silu(depthwise_conv_centred(xbc, conv_w, conv_b))
    gn = SSM_GROUPS * SSM_STATE
    xs = xbc[..., :SSM_D_INNER].reshape(b, S, SSM_HEADS, SSM_HEAD_DIM)
    Bm = xbc[..., SSM_D_INNER:SSM_D_INNER + gn].reshape(b, S, SSM_GROUPS, SSM_STATE)
    Cm = xbc[..., SSM_D_INNER + gn:].reshape(b, S, SSM_GROUPS, SSM_STATE)
    dt = jax.nn.softplus(dt_raw.astype(jnp.float32).reshape(b, S, 2, SSM_HEADS)
                         + dt_bias.astype(jnp.float32))
    A = -jnp.exp(a_log.astype(jnp.float32))
    y_f = ssd_scan(xs, dt[:, :, 0], A[0], Bm, Cm, strict=False)
    y_b = flip_seq(ssd_scan(flip_seq(xs), flip_seq(dt[:, :, 1]), A[1],
                            flip_seq(Bm), flip_seq(Cm), strict=True))
    y = y_f + y_b + xs * d_skip[:, None].astype(xs.dtype)
    y = y.reshape(b, S, SSM_D_INNER)
    y = rmsnorm(y * jax.nn.silu(z), norm_w)
    return y @ w_out


def gqa_axial_mixer(h, w_in, q_norm, k_norm, w_out, cos, sin):
    b, S, _ = h.shape
    nq = ATT_Q_HEADS * ATT_HEAD_DIM
    nk = ATT_KV_HEADS * ATT_HEAD_DIM
    proj = h @ w_in
    q = proj[..., :nq].reshape(b, S, ATT_Q_HEADS, ATT_HEAD_DIM)
    k = proj[..., nq:nq + nk].reshape(b, S, ATT_KV_HEADS, ATT_HEAD_DIM)
    v = proj[..., nq + nk:].reshape(b, S, ATT_KV_HEADS, ATT_HEAD_DIM)
    q = apply_axial_rope(rmsnorm(q, q_norm), cos, sin)
    k = apply_axial_rope(rmsnorm(k, k_norm), cos, sin)
    rep = ATT_Q_HEADS // ATT_KV_HEADS
    nblk = S // ATT_BLOCK
    qb = q.reshape(b, nblk, ATT_BLOCK, ATT_KV_HEADS, rep, ATT_HEAD_DIM).transpose(1, 0, 2, 3, 4, 5)
    scale = ATT_HEAD_DIM ** -0.5

    def block(q_blk):
        s = jnp.einsum("bqkrd,bskd->bkrqs", q_blk, k).astype(jnp.float32) * scale
        p = jax.nn.softmax(s, axis=-1)
        return jnp.einsum("bkrqs,bskd->bqkrd", p.astype(v.dtype), v)

    o = lax.map(block, qb)
    o = o.transpose(1, 0, 2, 3, 4, 5).reshape(b, S, nq)
    return o @ w_out


def retention_scan(q, k, v, log_gamma, strict):
    b, S, H, dk = q.shape
    dv = v.shape[-1]
    L = RET_CHUNK
    c = S // L
    qc = q.astype(jnp.float32).reshape(b, c, L, H, dk)
    kc = k.astype(jnp.float32).reshape(b, c, L, H, dk)
    vc = v.astype(jnp.float32).reshape(b, c, L, H, dv)
    pos = jnp.arange(L, dtype=jnp.float32)
    mask = jnp.tril(jnp.ones((L, L), dtype=bool), k=-1 if strict else 0)[..., None]
    diff = (pos[:, None] - pos[None, :])[..., None] * log_gamma
    dmat = jnp.where(mask, jnp.exp(jnp.where(mask, diff, 0.0)), 0.0)
    scores = jnp.einsum("bclhd,bcshd->bclsh", qc, kc) * dmat
    y_in = jnp.einsum("bclsh,bcshe->bclhe", scores, vc)
    k_dec = jnp.exp((L - 1 - pos)[:, None] * log_gamma)
    states = jnp.einsum("bclhd,lh,bclhe->bchde", kc, k_dec, vc)
    chunk_dec = jnp.exp(L * log_gamma)

    def step(r, st):
        return r * chunk_dec[:, None, None] + st, r

    r0 = jnp.zeros((b, H, dk, dv), jnp.float32)
    _, prev = lax.scan(step, r0, jnp.swapaxes(states, 0, 1))
    prev = jnp.swapaxes(prev, 0, 1)
    q_dec = jnp.exp((pos + 1)[:, None] * log_gamma)
    y_cross = jnp.einsum("bclhd,bchde,lh->bclhe", qc, prev, q_dec)
    return (y_in + y_cross).reshape(b, S, H, dv).astype(v.dtype)


def retention_mixer(h, w_in, log2_decay, w_out, cos, sin):
    b, S, _ = h.shape
    proj = h @ w_in
    q = proj[..., :D_MODEL].reshape(b, S, RET_HEADS, RET_QK_DIM)
    k = proj[..., D_MODEL:2 * D_MODEL].reshape(b, S, RET_HEADS, RET_QK_DIM)
    v = proj[..., 2 * D_MODEL:4 * D_MODEL].reshape(b, S, RET_HEADS, RET_V_DIM)
    g = proj[..., 4 * D_MODEL:]
    q = apply_axial_rope(q, cos, sin)
    k = apply_axial_rope(k, cos, sin) * (RET_QK_DIM ** -0.5)
    log_gamma = jnp.log1p(-jnp.exp2(log2_decay.astype(jnp.float32)))
    y_f = retention_scan(q, k, v, log_gamma[0], strict=False)
    y_b = flip_seq(retention_scan(flip_seq(q), flip_seq(k), flip_seq(v), log_gamma[1], strict=True))
    y = rmsnorm_plain(y_f + y_b).reshape(b, S, 2 * D_MODEL)
    return (jax.nn.silu(g) * y) @ w_out


def sqrelu_mlp(h, w1, w2):
    a = jax.nn.relu(h @ w1)
    return (a * a) @ w2


def setup_inputs(seed: int = 0) -> dict:
    key = jax.random.key(seed)
    ks = iter(jax.random.split(key, 32))
    f32 = jnp.float32

    def nrm(shape, scale):
        return jax.random.normal(next(ks), shape, f32) * scale

    def gain(shape):
        return 1.0 + 0.05 * jax.random.normal(next(ks), shape, f32)

    x = jax.random.normal(next(ks), (BATCH, SEQ, D_MODEL), f32)
    norm_mix_pre = gain((DEPTH, D_MODEL))
    norm_mix_post = gain((DEPTH, D_MODEL))
    norm_ffn_pre = gain((DEPTH, D_MODEL))
    norm_ffn_post = gain((DEPTH, D_MODEL))
    mlp_w_in = nrm((DEPTH, D_MODEL, D_FF), D_MODEL ** -0.5)
    mlp_w_out = nrm((DEPTH, D_FF, D_MODEL), D_FF ** -0.5)
    ssm_w_in = nrm((N_A, D_MODEL, SSM_IN_DIM), D_MODEL ** -0.5)
    ssm_conv_w = nrm((N_A, SSM_CONV, SSM_CONV_DIM), SSM_CONV ** -0.5)
    ssm_conv_b = nrm((N_A, SSM_CONV_DIM), 0.02)
    dt0 = jnp.exp(jax.random.uniform(next(ks), (N_A, 2, SSM_HEADS), f32,
                                     math.log(1e-3), math.log(1e-1)))
    ssm_dt_bias = dt0 + jnp.log(-jnp.expm1(-dt0))
    ssm_a_log = jnp.log(jax.random.uniform(next(ks), (N_A, 2, SSM_HEADS), f32, 1.0, 16.0))
    ssm_d = gain((N_A, SSM_HEADS))
    ssm_norm = gain((N_A, SSM_D_INNER))
    ssm_w_out = nrm((N_A, SSM_D_INNER, D_MODEL), SSM_D_INNER ** -0.5)
    att_w_in = nrm((N_B, D_MODEL, ATT_IN_DIM), D_MODEL ** -0.5)
    att_q_norm = gain((N_B, ATT_HEAD_DIM))
    att_k_norm = gain((N_B, ATT_HEAD_DIM))
    att_w_out = nrm((N_B, ATT_Q_HEADS * ATT_HEAD_DIM, D_MODEL), (ATT_Q_HEADS * ATT_HEAD_DIM) ** -0.5)
    ret_w_in = nrm((N_C, D_MODEL, RET_IN_DIM), D_MODEL ** -0.5)
    ret_log2_decay = (-5.0 - jnp.arange(RET_HEADS, dtype=f32))[None, None, :] \
        + 0.1 * jax.random.normal(next(ks), (N_C, 2, RET_HEADS), f32)
    ret_w_out = nrm((N_C, 2 * D_MODEL, D_MODEL), (2 * D_MODEL) ** -0.5)
    return {
        "x": x,
        "norm_mix_pre": norm_mix_pre, "norm_mix_post": norm_mix_post,
        "norm_ffn_pre": norm_ffn_pre, "norm_ffn_post": norm_ffn_post,
        "mlp_w_in": mlp_w_in, "mlp_w_out": mlp_w_out,
        "ssm_w_in": ssm_w_in, "ssm_conv_w": ssm_conv_w, "ssm_conv_b": ssm_conv_b,
        "ssm_dt_bias": ssm_dt_bias, "ssm_a_log": ssm_a_log, "ssm_d": ssm_d,
        "ssm_norm": ssm_norm, "ssm_w_out": ssm_w_out,
        "att_w_in": att_w_in, "att_q_norm": att_q_norm, "att_k_norm": att_k_norm,
        "att_w_out": att_w_out,
        "ret_w_in": ret_w_in, "ret_log2_decay": ret_log2_decay, "ret_w_out": ret_w_out,
    }


def reference(x, norm_mix_pre, norm_mix_post, norm_ffn_pre, norm_ffn_post, mlp_w_in, mlp_w_out,
              ssm_w_in, ssm_conv_w, ssm_conv_b, ssm_dt_bias, ssm_a_log, ssm_d, ssm_norm, ssm_w_out,
              att_w_in, att_q_norm, att_k_norm, att_w_out,
              ret_w_in, ret_log2_decay, ret_w_out):
    S = x.shape[1]
    cos_att, sin_att = axial_rope_tables(S, ATT_HEAD_DIM)
    cos_ret, sin_ret = axial_rope_tables(S, RET_QK_DIM)
    for i in range(DEPTH):
        kind = i % N_MIXERS
        j = i // N_MIXERS
        hn = rmsnorm(x, norm_mix_pre[i])
        if kind == 0:
            m = mamba2_mixer(hn, ssm_w_in[j], ssm_conv_w[j], ssm_conv_b[j], ssm_dt_bias[j],
                             ssm_a_log[j], ssm_d[j], ssm_norm[j], ssm_w_out[j])
        elif kind == 1:
            m = gqa_axial_mixer(hn, att_w_in[j], att_q_norm[j], att_k_norm[j], att_w_out[j],
                                cos_att, sin_att)
        else:
            m = retention_mixer(hn, ret_w_in[j], ret_log2_decay[j], ret_w_out[j], cos_ret, sin_ret)
        x = x + rmsnorm(m, norm_mix_post[i]).astype(x.dtype)
        hn = rmsnorm(x, norm_ffn_pre[i])
        x = x + rmsnorm(sqrelu_mlp(hn, mlp_w_in[i], mlp_w_out[i]), norm_ffn_post[i]).astype(x.dtype)
    return x
```

```python
import functools
import math

import jax
import jax.numpy as jnp
from jax import lax
from jax.experimental import pallas as pl
from jax.experimental.pallas import tpu as pltpu

F32 = jnp.float32
BF16 = jnp.bfloat16

EPS = 1e-6
GRID_W = 64
ROPE_THETA = 10000.0

D_MODEL = 1024
D_FF = 4 * D_MODEL

SSM_D_INNER = 2 * D_MODEL
SSM_HEAD_DIM = 64
SSM_HEADS = SSM_D_INNER // SSM_HEAD_DIM
SSM_GROUPS = 8
SSM_HEADS_PER_GROUP = SSM_HEADS // SSM_GROUPS
SSM_STATE = 128
SSM_CONV = 5
SSM_CHUNK = 128
SSM_GN = SSM_GROUPS * SSM_STATE
SSM_CONV_DIM = SSM_D_INNER + 2 * SSM_GN
SSM_PROJ_MAIN = SSM_D_INNER + SSM_CONV_DIM
SSM_DT_PAD = 128
SSM_PROJ_COLS = 6400

ATT_HEAD_DIM = 128
ATT_Q_HEADS = D_MODEL // ATT_HEAD_DIM
ATT_KV_HEADS = 2
ATT_REP = ATT_Q_HEADS // ATT_KV_HEADS
ATT_IN_DIM = (ATT_Q_HEADS + 2 * ATT_KV_HEADS) * ATT_HEAD_DIM

RET_HEADS = 4
RET_QK_DIM = D_MODEL // RET_HEADS
RET_V_DIM = 2 * RET_QK_DIM
RET_CHUNK = 128

LANES = 128
SUBLANES = 8
VMEM_LIMIT = 56 * 1024 * 1024


def _cparams(semantics):
    return pltpu.CompilerParams(dimension_semantics=semantics, vmem_limit_bytes=VMEM_LIMIT)


def _rms(x):
    return x * lax.rsqrt(jnp.mean(x * x, axis=-1, keepdims=True) + EPS)


def _silu(x):
    return x * jax.nn.sigmoid(x)


def _softplus(x):
    return jnp.maximum(x, 0.0) + jnp.log1p(jnp.exp(-jnp.abs(x)))


def _dot(a, b):
    return jnp.dot(a, b, preferred_element_type=F32)


def _dot_nt(a, b):
    return lax.dot_general(a, b, (((1,), (1,)), ((), ())), preferred_element_type=F32)


def _dot_tn(a, b):
    return lax.dot_general(a, b, (((0,), (0,)), ((), ())), preferred_element_type=F32)


def _rope_piece(h, cos, sin, nf):
    if 2 * nf == LANES:
        swapped = pltpu.roll(h, nf, 1)
    else:
        lane = lax.broadcasted_iota(jnp.int32, h.shape, 1)
        first = (lane % (2 * nf)) < nf
        swapped = jnp.where(first, pltpu.roll(h, LANES - nf, 1), pltpu.roll(h, nf, 1))
    return h * cos + swapped * sin


def _mlp_kernel(x_ref, g1_ref, w1_ref, w2_ref, g2_ref, o_ref, xn_ref, acc_ref):
    j = pl.program_id(1)

    @pl.when(j == 0)
    def _():
        xn_ref[...] = (_rms(x_ref[...]) * g1_ref[...]).astype(BF16)
        acc_ref[...] = jnp.zeros_like(acc_ref)

    h = jnp.maximum(_dot(xn_ref[...], w1_ref[...]), 0.0)
    acc_ref[...] += _dot((h * h).astype(BF16), w2_ref[...])

    @pl.when(j == pl.num_programs(1) - 1)
    def _():
        o_ref[...] = x_ref[...] + _rms(acc_ref[...]) * g2_ref[...]


def _mlp(x, g1, w1, w2, g2, tm, tf):
    T, D = x.shape
    F = w1.shape[1]
    return pl.pallas_call(
        _mlp_kernel,
        grid=(T // tm, F // tf),
        in_specs=[
            pl.BlockSpec((tm, D), lambda i, j: (i, 0)),
            pl.BlockSpec((1, D), lambda i, j: (0, 0)),
            pl.BlockSpec((D, tf), lambda i, j: (0, j)),
            pl.BlockSpec((tf, D), lambda i, j: (j, 0)),
            pl.BlockSpec((1, D), lambda i, j: (0, 0)),
        ],
        out_specs=pl.BlockSpec((tm, D), lambda i, j: (i, 0)),
        out_shape=jax.ShapeDtypeStruct((T, D), F32),
        scratch_shapes=[pltpu.VMEM((tm, D), BF16), pltpu.VMEM((tm, D), F32)],
        compiler_params=_cparams(("parallel", "arbitrary")),
        name="mlp",
    )(x, g1, w1, w2, g2)


def _norm_proj_kernel(x_ref, g_ref, w_ref, o_ref, xn_ref):
    @pl.when(pl.program_id(1) == 0)
    def _():
        xn_ref[...] = (_rms(x_ref[...]) * g_ref[...]).astype(BF16)

    o_ref[...] = _dot(xn_ref[...], w_ref[...]).astype(o_ref.dtype)


def _norm_proj(x, g, w, tm, tn, out_dtype, name):
    T, D = x.shape
    N = w.shape[1]
    return pl.pallas_call(
        _norm_proj_kernel,
        grid=(T // tm, N // tn),
        in_specs=[
            pl.BlockSpec((tm, D), lambda i, j: (i, 0)),
            pl.BlockSpec((1, D), lambda i, j: (0, 0)),
            pl.BlockSpec((D, tn), lambda i, j: (0, j)),
        ],
        out_specs=pl.BlockSpec((tm, tn), lambda i, j: (i, j)),
        out_shape=jax.ShapeDtypeStruct((T, N), out_dtype),
        scratch_shapes=[pltpu.VMEM((tm, D), BF16)],
        compiler_params=_cparams(("parallel", "arbitrary")),
        name=name,
    )(x, g, w)


def _att_inproj_kernel(x_ref, g_ref, w_ref, qn_ref, kn_ref, cq_ref, sq_ref, ck_ref, sk_ref, o_ref):
    xn = (_rms(x_ref[...]) * g_ref[...]).astype(BF16)
    nf = ATT_HEAD_DIM // 4
    for hh in range(ATT_Q_HEADS + ATT_KV_HEADS):
        c0 = hh * ATT_HEAD_DIM
        h = _dot(xn, w_ref[:, c0:c0 + ATT_HEAD_DIM])
        if hh < ATT_Q_HEADS:
            h = _rope_piece(_rms(h) * qn_ref[...], cq_ref[...], sq_ref[...], nf)
        else:
            h = _rope_piece(_rms(h) * kn_ref[...], ck_ref[...], sk_ref[...], nf)
        o_ref[:, c0:c0 + ATT_HEAD_DIM] = h.astype(o_ref.dtype)
    c0 = (ATT_Q_HEADS + ATT_KV_HEADS) * ATT_HEAD_DIM
    o_ref[:, c0:] = _dot(xn, w_ref[:, c0:]).astype(o_ref.dtype)


def _att_inproj(x, g, w, qn, kn, cq, sq, ck, sk, S, tm):
    T, D = x.shape
    N = w.shape[1]
    nS = S // tm
    const = lambda i: (0, 0)
    tab = pl.BlockSpec((tm, ATT_HEAD_DIM), lambda i: (i % nS, 0))
    return pl.pallas_call(
        _att_inproj_kernel,
        grid=(T // tm,),
        in_specs=[
            pl.BlockSpec((tm, D), lambda i: (i, 0)),
            pl.BlockSpec((1, D), const),
            pl.BlockSpec((D, N), const),
            pl.BlockSpec((1, ATT_HEAD_DIM), const),
            pl.BlockSpec((1, ATT_HEAD_DIM), const),
            tab, tab, tab, tab,
        ],
        out_specs=pl.BlockSpec((tm, N), lambda i: (i, 0)),
        out_shape=jax.ShapeDtypeStruct((T, N), BF16),
        compiler_params=_cparams(("parallel",)),
        name="att_inproj",
    )(x, g, w, qn, kn, cq, sq, ck, sk)


def _flash_kernel(q_ref, k_ref, v_ref, o_ref, m_ref, l_ref, acc_ref):
    ki = pl.program_id(3)
    hd = ATT_HEAD_DIM

    @pl.when(ki == 0)
    def _():
        m_ref[...] = jnp.full_like(m_ref, -jnp.inf)
        l_ref[...] = jnp.zeros_like(l_ref)
        acc_ref[...] = jnp.zeros_like(acc_ref)

    k = k_ref[...]
    v = v_ref[...]
    for r in range(ATT_REP):
        s = _dot_nt(q_ref[:, r * hd:(r + 1) * hd], k)
        m_prev = m_ref[r]
        m_new = jnp.maximum(m_prev, jnp.max(s, axis=1, keepdims=True))
        alpha = jnp.exp(m_prev - m_new)
        p = jnp.exp(s - m_new)
        l_ref[r] = alpha * l_ref[r] + jnp.sum(p, axis=1, keepdims=True)
        acc_ref[r] = alpha * acc_ref[r] + _dot(p.astype(BF16), v)
        m_ref[r] = m_new

    @pl.when(ki == pl.num_programs(3) - 1)
    def _():
        for r in range(ATT_REP):
            o_ref[:, r * hd:(r + 1) * hd] = (acc_ref[r] / l_ref[r]).astype(o_ref.dtype)


def _flash(qkv, B, S, tq, tk):
    T = B * S
    hd = ATT_HEAD_DIM
    nq, nk = S // tq, S // tk
    koff = ATT_Q_HEADS
    voff = ATT_Q_HEADS + ATT_KV_HEADS
    return pl.pallas_call(
        _flash_kernel,
        grid=(B, ATT_KV_HEADS, nq, nk),
        in_specs=[
            pl.BlockSpec((tq, ATT_REP * hd), lambda b, h, qi, ki: (b * nq + qi, h)),
            pl.BlockSpec((tk, hd), lambda b, h, qi, ki: (b * nk + ki, koff + h)),
            pl.BlockSpec((tk, hd), lambda b, h, qi, ki: (b * nk + ki, voff + h)),
        ],
        out_specs=pl.BlockSpec((tq, ATT_REP * hd), lambda b, h, qi, ki: (b * nq + qi, h)),
        out_shape=jax.ShapeDtypeStruct((T, ATT_Q_HEADS * hd), BF16),
        scratch_shapes=[
            pltpu.VMEM((ATT_REP, tq, 1), F32),
            pltpu.VMEM((ATT_REP, tq, 1), F32),
            pltpu.VMEM((ATT_REP, tq, hd), F32),
        ],
        compiler_params=_cparams(("parallel", "parallel", "parallel", "arbitrary")),
        name="flash",
    )(qkv, qkv, qkv)


def _att_out_kernel(o_ref, w_ref, g_ref, x_ref, out_ref):
    out_ref[...] = x_ref[...] + _rms(_dot(o_ref[...], w_ref[...])) * g_ref[...]


def _att_out(o, w, g, x, tm):
    T, D = x.shape
    K = o.shape[1]
    return pl.pallas_call(
        _att_out_kernel,
        grid=(T // tm,),
        in_specs=[
            pl.BlockSpec((tm, K), lambda i: (i, 0)),
            pl.BlockSpec((K, D), lambda i: (0, 0)),
            pl.BlockSpec((1, D), lambda i: (0, 0)),
            pl.BlockSpec((tm, D), lambda i: (i, 0)),
        ],
        out_specs=pl.BlockSpec((tm, D), lambda i: (i, 0)),
        out_shape=jax.ShapeDtypeStruct((T, D), F32),
        compiler_params=_cparams(("parallel",)),
        name="att_out",
    )(o, w, g, x)


def _conv_kernel(prev_ref, cur_ref, next_ref, w_ref, b_ref, o_ref, ext_ref, *, tiles_per_seq):
    i = pl.program_id(0)
    tm = cur_ref.shape[0]
    pad = SSM_CONV // 2
    first = (i % tiles_per_seq) == 0
    last = (i % tiles_per_seq) == tiles_per_seq - 1
    ext_ref[0:SUBLANES, :] = jnp.where(first, 0.0, prev_ref[...])
    ext_ref[SUBLANES:SUBLANES + tm, :] = cur_ref[...]
    ext_ref[SUBLANES + tm:, :] = jnp.where(last, 0.0, next_ref[...])
    acc = jnp.broadcast_to(b_ref[...], o_ref.shape)
    for k in range(SSM_CONV):
        acc = acc + w_ref[k:k + 1, :] * ext_ref[pl.ds(SUBLANES - pad + k, tm), :]
    o_ref[...] = _silu(acc)


def _ssm_conv(proj, conv_w, conv_b, S, tm, tc):
    T = proj.shape[0]
    C = SSM_CONV_DIM
    c0 = SSM_D_INNER // tc
    rows8 = tm // SUBLANES
    nblk8 = T // SUBLANES
    return pl.pallas_call(
        functools.partial(_conv_kernel, tiles_per_seq=S // tm),
        grid=(T // tm, C // tc),
        in_specs=[
            pl.BlockSpec((SUBLANES, tc), lambda i, j: (jnp.maximum(i * rows8 - 1, 0), c0 + j)),
            pl.BlockSpec((tm, tc), lambda i, j: (i, c0 + j)),
            pl.BlockSpec((SUBLANES, tc), lambda i, j: (jnp.minimum((i + 1) * rows8, nblk8 - 1), c0 + j)),
            pl.BlockSpec((SSM_CONV, tc), lambda i, j: (0, j)),
            pl.BlockSpec((1, tc), lambda i, j: (0, j)),
        ],
        out_specs=pl.BlockSpec((tm, tc), lambda i, j: (i, j)),
        out_shape=jax.ShapeDtypeStruct((T, C), F32),
        scratch_shapes=[pltpu.VMEM((tm + 2 * SUBLANES, tc), F32)],
        compiler_params=_cparams(("parallel", "parallel")),
        name="ssm_conv",
    )(proj, proj, proj, conv_w, conv_b)


def _ssd_scan_kernel(x_ref, b_ref, c_ref, dt_ref, bias_ref, alog_ref, y_ref, st_ref, *, reverse, col0):
    L = SSM_CHUNK
    N = SSM_STATE
    P = SSM_HEAD_DIM

    @pl.when(pl.program_id(1) == 0)
    def _():
        st_ref[...] = jnp.zeros_like(st_ref)

    dt = _softplus(dt_ref[...] + bias_ref[...])
    a = dt * (-jnp.exp(alog_ref[...]))
    li = lax.broadcasted_iota(jnp.int32, (L, L), 0)
    si = lax.broadcasted_iota(jnp.int32, (L, L), 1)
    tri = (si >= li) if reverse else (si <= li)
    cum = jnp.dot(tri.astype(F32), a, precision=lax.Precision.HIGHEST, preferred_element_type=F32)
    cum_t = cum.T
    dt_t = dt.T
    mask = (si > li) if reverse else (si <= li)
    tot = 0 if reverse else L - 1

    for g in range(SSM_GROUPS):
        b_g = b_ref[:, g * N:(g + 1) * N]
        c_g = c_ref[:, g * N:(g + 1) * N]
        cb = _dot_nt(c_g.astype(BF16), b_g.astype(BF16))
        b_t = b_g.T
        for j in range(SSM_HEADS_PER_GROUP):
            h = g * SSM_HEADS_PER_GROUP + j
            col = col0 + h
            c_col = cum[:, col:col + 1]
            c_row = cum_t[col:col + 1, :]
            dt_row = dt_t[col:col + 1, :]
            w = cb * jnp.where(mask, jnp.exp(c_col - c_row), 0.0) * dt_row
            ec = c_g * jnp.exp(c_col)
            lhs = jnp.concatenate([w, ec], axis=1).astype(BF16)
            xh = x_ref[:, h * P:(h + 1) * P].astype(BF16)
            st = st_ref[h]
            rhs = jnp.concatenate([xh, st.astype(BF16)], axis=0)
            y_ref[:, h * P:(h + 1) * P] = _dot(lhs, rhs)
            c_tot = c_row[:, tot:tot + 1]
            dte = dt_row * jnp.exp(c_tot - c_row)
            st_ref[h] = st * jnp.exp(c_tot) + _dot((b_t * dte).astype(BF16), xh)


def _ssd_scan(xbc, proj, dt_bias, a_log, B, S, reverse):
    T = B * S
    L = SSM_CHUNK
    nC = S // L
    col0 = SSM_HEADS if reverse else 0
    dt_blk = SSM_PROJ_MAIN // SSM_DT_PAD

    def row(b, c):
        return b * nC + ((nC - 1 - c) if reverse else c)

    return pl.pallas_call(
        functools.partial(_ssd_scan_kernel, reverse=reverse, col0=col0),
        grid=(B, nC),
        in_specs=[
            pl.BlockSpec((L, SSM_D_INNER), lambda b, c: (row(b, c), 0)),
            pl.BlockSpec((L, SSM_GN), lambda b, c: (row(b, c), SSM_D_INNER // SSM_GN)),
            pl.BlockSpec((L, SSM_GN), lambda b, c: (row(b, c), SSM_D_INNER // SSM_GN + 1)),
            pl.BlockSpec((L, SSM_DT_PAD), lambda b, c: (row(b, c), dt_blk)),
            pl.BlockSpec((1, SSM_DT_PAD), lambda b, c: (0, 0)),
            pl.BlockSpec((1, SSM_DT_PAD), lambda b, c: (0, 0)),
        ],
        out_specs=pl.BlockSpec((L, SSM_D_INNER), lambda b, c: (row(b, c), 0)),
        out_shape=jax.ShapeDtypeStruct((T, SSM_D_INNER), F32),
        scratch_shapes=[pltpu.VMEM((SSM_HEADS, SSM_STATE, SSM_HEAD_DIM), F32)],
        compiler_params=_cparams(("parallel", "arbitrary")),
        name="ssd_scan_bwd" if reverse else "ssd_scan_fwd",
    )(xbc, xbc, xbc, proj, dt_bias, a_log)


def _ssm_out_kernel(yf_ref, yb_ref, xs_ref, z_ref, d_ref, nw_ref, w_ref, g_ref, x_ref, out_ref):
    y = yf_ref[...] + yb_ref[...] + xs_ref[...] * d_ref[...]
    y = _rms(y * _silu(z_ref[...])) * nw_ref[...]
    out_ref[...] = x_ref[...] + _rms(_dot(y.astype(BF16), w_ref[...])) * g_ref[...]


def _ssm_out(yf, yb, xbc, proj, d_exp, norm_w, w_out, g, x, tm):
    T, D = x.shape
    K = SSM_D_INNER
    tok = lambda i: (i, 0)
    const = lambda i: (0, 0)
    return pl.pallas_call(
        _ssm_out_kernel,
        grid=(T // tm,),
        in_specs=[
            pl.BlockSpec((tm, K), tok),
            pl.BlockSpec((tm, K), tok),
            pl.BlockSpec((tm, K), tok),
            pl.BlockSpec((tm, K), tok),
            pl.BlockSpec((1, K), const),
            pl.BlockSpec((1, K), const),
            pl.BlockSpec((K, D), const),
            pl.BlockSpec((1, D), const),
            pl.BlockSpec((tm, D), tok),
        ],
        out_specs=pl.BlockSpec((tm, D), tok),
        out_shape=jax.ShapeDtypeStruct((T, D), F32),
        compiler_params=_cparams(("parallel",)),
        name="ssm_out",
    )(yf, yb, xbc, proj, d_exp, norm_w, w_out, g, x)


def _ret_qk_kernel(x_ref, g_ref, w_ref, cos_ref, sin_ref, o_ref, xn_ref):
    @pl.when(pl.program_id(1) == 0)
    def _():
        xn_ref[...] = (_rms(x_ref[...]) * g_ref[...]).astype(BF16)

    nf = RET_QK_DIM // 4
    xn = xn_ref[...]
    for p in range(o_ref.shape[1] // LANES):
        c0 = p * LANES
        t0 = (p % (RET_QK_DIM // LANES)) * LANES
        h = _dot(xn, w_ref[:, c0:c0 + LANES])
        h = _rope_piece(h, cos_ref[0, :, t0:t0 + LANES], sin_ref[0, :, t0:t0 + LANES], nf)
        o_ref[:, c0:c0 + LANES] = h.astype(o_ref.dtype)


def _ret_qk(x, g, w, cos, sin, S, tm):
    T, D = x.shape
    nS = S // tm
    return pl.pallas_call(
        _ret_qk_kernel,
        grid=(T // tm, 2),
        in_specs=[
            pl.BlockSpec((tm, D), lambda i, j: (i, 0)),
            pl.BlockSpec((1, D), lambda i, j: (0, 0)),
            pl.BlockSpec((D, D), lambda i, j: (0, j)),
            pl.BlockSpec((1, tm, RET_QK_DIM), lambda i, j: (j, i % nS, 0)),
            pl.BlockSpec((1, tm, RET_QK_DIM), lambda i, j: (j, i % nS, 0)),
        ],
        out_specs=pl.BlockSpec((tm, D), lambda i, j: (i, j)),
        out_shape=jax.ShapeDtypeStruct((T, 2 * D), BF16),
        scratch_shapes=[pltpu.VMEM((tm, D), BF16)],
        compiler_params=_cparams(("parallel", "arbitrary")),
        name="ret_qk",
    )(x, g, w, cos, sin)


def _ret_scan_kernel(qk_ref, v_ref, p_ref, y_ref, r_ref, *, reverse, row0):
    L = RET_CHUNK
    dk, dv = RET_QK_DIM, RET_V_DIM
    rep = dv // LANES

    @pl.when(pl.program_id(1) == 0)
    def _():
        r_ref[...] = jnp.zeros_like(r_ref)

    li = lax.broadcasted_iota(jnp.int32, (L, LANES), 0)
    si = lax.broadcasted_iota(jnp.int32, (L, LANES), 1)
    lf = li.astype(F32)
    if reverse:
        mask = si > li
        diff = (si - li).astype(F32)
        q_pow = L - lf
        k_pow = lf
    else:
        mask = si <= li
        diff = (li - si).astype(F32)
        q_pow = lf + 1.0
        k_pow = L - 1.0 - lf

    for h in range(RET_HEADS):
        lg = jnp.log1p(-jnp.exp2(p_ref[row0 + h:row0 + h + 1, :]))
        dmat = jnp.where(mask, jnp.exp(diff * lg), 0.0)
        q_dec = jnp.exp(q_pow * lg)
        k_dec = jnp.exp(k_pow * lg)
        q = qk_ref[:, h * dk:(h + 1) * dk]
        k = qk_ref[:, RET_HEADS * dk + h * dk:RET_HEADS * dk + (h + 1) * dk]
        v = v_ref[:, h * dv:(h + 1) * dv]
        r = r_ref[h]
        scores = (_dot_nt(q, k) * dmat).astype(BF16)
        y_in = _dot(scores, v)
        y_cross = _dot(q, r.astype(BF16)) * jnp.concatenate([q_dec] * rep, axis=1)
        y_ref[:, h * dv:(h + 1) * dv] = y_in + y_cross
        v_sc = (v.astype(F32) * jnp.concatenate([k_dec] * rep, axis=1)).astype(BF16)
        r_ref[h] = r * jnp.exp(float(L) * lg[:, :1]) + _dot_tn(k, v_sc)


def _ret_scan(qk, v, p_rows, B, S, reverse):
    T = B * S
    L = RET_CHUNK
    nC = S // L
    row0 = RET_HEADS if reverse else 0

    def row(b, c):
        return b * nC + ((nC - 1 - c) if reverse else c)

    return pl.pallas_call(
        functools.partial(_ret_scan_kernel, reverse=reverse, row0=row0),
        grid=(B, nC),
        in_specs=[
            pl.BlockSpec((L, 2 * D_MODEL), lambda b, c: (row(b, c), 0)),
            pl.BlockSpec((L, RET_HEADS * RET_V_DIM), lambda b, c: (row(b, c), 0)),
            pl.BlockSpec((2 * RET_HEADS, LANES), lambda b, c: (0, 0)),
        ],
        out_specs=pl.BlockSpec((L, RET_HEADS * RET_V_DIM), lambda b, c: (row(b, c), 0)),
        out_shape=jax.ShapeDtypeStruct((T, RET_HEADS * RET_V_DIM), F32),
        scratch_shapes=[pltpu.VMEM((RET_HEADS, RET_QK_DIM, RET_V_DIM), F32)],
        compiler_params=_cparams(("parallel", "arbitrary")),
        name="ret_scan_bwd" if reverse else "ret_scan_fwd",
    )(qk, v, p_rows)


def _ret_out_kernel(yf_ref, yb_ref, gate_ref, w_ref, g_ref, x_ref, out_ref):
    dv = RET_V_DIM
    pieces = []
    for h in range(RET_HEADS):
        y = yf_ref[:, h * dv:(h + 1) * dv] + yb_ref[:, h * dv:(h + 1) * dv]
        pieces.append((_silu(gate_ref[:, h * dv:(h + 1) * dv]) * _rms(y)).astype(BF16))
    y = jnp.concatenate(pieces, axis=1)
    out_ref[...] = x_ref[...] + _rms(_dot(y, w_ref[...])) * g_ref[...]


def _ret_out(yf, yb, gate, w_out, g, x, tm):
    T, D = x.shape
    K = RET_HEADS * RET_V_DIM
    tok = lambda i: (i, 0)
    const = lambda i: (0, 0)
    return pl.pallas_call(
        _ret_out_kernel,
        grid=(T // tm,),
        in_specs=[
            pl.BlockSpec((tm, K), tok),
            pl.BlockSpec((tm, K), tok),
            pl.BlockSpec((tm, K), tok),
            pl.BlockSpec((K, D), const),
            pl.BlockSpec((1, D), const),
            pl.BlockSpec((tm, D), tok),
        ],
        out_specs=pl.BlockSpec((tm, D), tok),
        out_shape=jax.ShapeDtypeStruct((T, D), F32),
        compiler_params=_cparams(("parallel",)),
        name="ret_out",
    )(yf, yb, gate, w_out, g, x)


def _rope_tables(S, dim, scale):
    rows = S // GRID_W
    row = jnp.repeat(jnp.arange(rows, dtype=F32), GRID_W)
    col = jnp.tile(jnp.arange(GRID_W, dtype=F32), rows)
    half = dim // 2
    nf = half // 2
    inv = ROPE_THETA ** (-(jnp.arange(nf, dtype=F32) * 2.0) / half)
    ar = row[:, None] * inv
    ac = col[:, None] * inv
    cos = jnp.concatenate([jnp.cos(ar), jnp.cos(ar), jnp.cos(ac), jnp.cos(ac)], axis=1)
    sin = jnp.concatenate([-jnp.sin(ar), jnp.sin(ar), -jnp.sin(ac), jnp.sin(ac)], axis=1)
    return cos * scale, sin * scale


def _row(v):
    return v.reshape(1, -1).astype(F32)


def _tile(n, pref):
    return pref if n % pref == 0 else n


def _ssm_layer(x, B, S, g_pre, g_post, w_in, conv_w, conv_b, dt_bias, a_log, d_skip, norm_w, w_out):
    T = B * S
    pad = SSM_PROJ_COLS - SSM_PROJ_MAIN - 2 * SSM_HEADS
    w_cat = jnp.pad(w_in, ((0, 0), (0, pad))).astype(BF16)
    proj = _norm_proj(x, _row(g_pre), w_cat, _tile(T, 1024), 1280, F32, "ssm_inproj")
    xbc = _ssm_conv(proj, conv_w.astype(F32), _row(conv_b), S, _tile(S, 512), 512)
    lane_pad = SSM_DT_PAD - 2 * SSM_HEADS
    bias = jnp.pad(dt_bias.reshape(1, -1).astype(F32), ((0, 0), (0, lane_pad)))
    alog = jnp.pad(a_log.reshape(1, -1).astype(F32), ((0, 0), (0, lane_pad)))
    yf = _ssd_scan(xbc, proj, bias, alog, B, S, reverse=False)
    yb = _ssd_scan(xbc, proj, bias, alog, B, S, reverse=True)
    d_exp = jnp.repeat(d_skip.astype(F32), SSM_HEAD_DIM).reshape(1, -1)
    return _ssm_out(yf, yb, xbc, proj, d_exp, _row(norm_w), w_out.astype(BF16), _row(g_post), x,
                    _tile(T, 512))


def _att_layer(x, B, S, g_pre, g_post, w_in, q_norm, k_norm, w_out):
    T = B * S
    cq, sq = _rope_tables(S, ATT_HEAD_DIM, ATT_HEAD_DIM ** -0.5)
    ck, sk = _rope_tables(S, ATT_HEAD_DIM, 1.0)
    qkv = _att_inproj(x, _row(g_pre), w_in.astype(BF16), _row(q_norm), _row(k_norm),
                      cq, sq, ck, sk, S, _tile(S, 512))
    o = _flash(qkv, B, S, _tile(S, 512), _tile(S, 1024))
    return _att_out(o, w_out.astype(BF16), _row(g_post), x, _tile(T, 1024))


def _ret_layer(x, B, S, g_pre, g_post, w_in, log2_decay, w_out):
    T = B * S
    D = D_MODEL
    cq, sq = _rope_tables(S, RET_QK_DIM, 1.0)
    ck, sk = _rope_tables(S, RET_QK_DIM, RET_QK_DIM ** -0.5)
    cos = jnp.stack([cq, ck])
    sin = jnp.stack([sq, sk])
    w = w_in.astype(BF16)
    g = _row(g_pre)
    qk = _ret_qk(x, g, w[:, :2 * D], cos, sin, S, _tile(S, 1024))
    v = _norm_proj(x, g, w[:, 2 * D:4 * D], _tile(T, 1024), 1024, BF16, "ret_v")
    gate = _norm_proj(x, g, w[:, 4 * D:], _tile(T, 1024), 1024, F32, "ret_gate")
    p_rows = jnp.broadcast_to(log2_decay.reshape(-1, 1).astype(F32), (2 * RET_HEADS, LANES))
    yf = _ret_scan(qk, v, p_rows, B, S, reverse=False)
    yb = _ret_scan(qk, v, p_rows, B, S, reverse=True)
    return _ret_out(yf, yb, gate, w_out.astype(BF16), _row(g_post), x, _tile(T, 512))


def kernel(x, norm_mix_pre, norm_mix_post, norm_ffn_pre, norm_ffn_post, mlp_w_in, mlp_w_out,
           ssm_w_in, ssm_conv_w, ssm_conv_b, ssm_dt_bias, ssm_a_log, ssm_d, ssm_norm, ssm_w_out,
           att_w_in, att_q_norm, att_k_norm, att_w_out,
           ret_w_in, ret_log2_decay, ret_w_out):
    B, S, D = x.shape
    T = B * S
    depth = norm_mix_pre.shape[0]
    h = x.reshape(T, D)
    for i in range(depth):
        kind, j = i % 3, i // 3
        if kind == 0:
            h = _ssm_layer(h, B, S, norm_mix_pre[i], norm_mix_post[i], ssm_w_in[j], ssm_conv_w[j],
                           ssm_conv_b[j], ssm_dt_bias[j], ssm_a_log[j], ssm_d[j], ssm_norm[j],
                           ssm_w_out[j])
        elif kind == 1:
            h = _att_layer(h, B, S, norm_mix_pre[i], norm_mix_post[i], att_w_in[j], att_q_norm[j],
                           att_k_norm[j], att_w_out[j])
        else:
            h = _ret_layer(h, B, S, norm_mix_pre[i], norm_mix_post[i], ret_w_in[j],
                           ret_log2_decay[j], ret_w_out[j])
        h = _mlp(h, _row(norm_ffn_pre[i]), mlp_w_in[i].astype(BF16), mlp_w_out[i].astype(BF16),
                 _row(norm_ffn_post[i]), _tile(T, 1024), 1024)
    return h.reshape(B, S, D)
```

```python
import functools
import math

import jax
import jax.numpy as jnp
from jax import lax
from jax.experimental import pallas as pl
from jax.experimental.pallas import tpu as pltpu

F32 = jnp.float32
BF16 = jnp.bfloat16

EPS = 1e-6
GRID_W = 64
ROPE_THETA = 10000.0

D_MODEL = 1024
D_FF = 4 * D_MODEL

SSM_D_INNER = 2 * D_MODEL
SSM_HEAD_DIM = 64
SSM_HEADS = SSM_D_INNER // SSM_HEAD_DIM
SSM_GROUPS = 8
SSM_HEADS_PER_GROUP = SSM_HEADS // SSM_GROUPS
SSM_STATE = 128
SSM_CONV = 5
SSM_CHUNK = 128
SSM_GN = SSM_GROUPS * SSM_STATE
SSM_CONV_DIM = SSM_D_INNER + 2 * SSM_GN
SSM_DT_PAD = 128
SSM_ZDT_COLS = SSM_D_INNER + 2 * SSM_DT_PAD

ATT_HEAD_DIM = 128
ATT_Q_HEADS = D_MODEL // ATT_HEAD_DIM
ATT_KV_HEADS = 2
ATT_REP = ATT_Q_HEADS // ATT_KV_HEADS
ATT_IN_DIM = (ATT_Q_HEADS + 2 * ATT_KV_HEADS) * ATT_HEAD_DIM

RET_HEADS = 4
RET_QK_DIM = D_MODEL // RET_HEADS
RET_V_DIM = 2 * RET_QK_DIM
RET_CHUNK = 128

LANES = 128
SUBLANES = 8
VMEM_LIMIT = 56 * 1024 * 1024


def _cparams(semantics):
    return pltpu.CompilerParams(dimension_semantics=semantics, vmem_limit_bytes=VMEM_LIMIT)


def _rms(x):
    return x * lax.rsqrt(jnp.mean(x * x, axis=-1, keepdims=True) + EPS)


def _silu(x):
    return x * jax.nn.sigmoid(x)


def _softplus(x):
    return jnp.maximum(x, 0.0) + jnp.log1p(jnp.exp(-jnp.abs(x)))


def _dot(a, b):
    return jnp.dot(a, b, preferred_element_type=F32)


def _dot_nt(a, b):
    return lax.dot_general(a, b, (((1,), (1,)), ((), ())), preferred_element_type=F32)


def _dot_tn(a, b):
    return lax.dot_general(a, b, (((0,), (0,)), ((), ())), preferred_element_type=F32)


def _rope_piece(h, cos, sin, nf):
    if 2 * nf == LANES:
        swapped = pltpu.roll(h, nf, 1)
    else:
        lane = lax.broadcasted_iota(jnp.int32, h.shape, 1)
        first = (lane % (2 * nf)) < nf
        swapped = jnp.where(first, pltpu.roll(h, LANES - nf, 1), pltpu.roll(h, nf, 1))
    return h * cos + swapped * sin


def _mlp_kernel(x_ref, g1_ref, w1_ref, w2_ref, g2_ref, o_ref, xn_ref, acc_ref):
    j = pl.program_id(1)

    @pl.when(j == 0)
    def _():
        xn_ref[...] = (_rms(x_ref[...]) * g1_ref[...]).astype(BF16)
        acc_ref[...] = jnp.zeros_like(acc_ref)

    xn = xn_ref[...]
    half = w1_ref.shape[1] // 2
    hs = [jnp.maximum(_dot(xn, w1_ref[:, c * half:(c + 1) * half]), 0.0) for c in range(2)]
    a = jnp.concatenate([(h * h).astype(BF16) for h in hs], axis=1)
    acc_ref[...] += _dot(a, w2_ref[...])

    @pl.when(j == pl.num_programs(1) - 1)
    def _():
        o_ref[...] = x_ref[...] + _rms(acc_ref[...]) * g2_ref[...]


def _mlp(x, g1, w1, w2, g2, tm, tf):
    T, D = x.shape
    F = w1.shape[1]
    return pl.pallas_call(
        _mlp_kernel,
        grid=(T // tm, F // tf),
        in_specs=[
            pl.BlockSpec((tm, D), lambda i, j: (i, 0)),
            pl.BlockSpec((1, D), lambda i, j: (0, 0)),
            pl.BlockSpec((D, tf), lambda i, j: (0, j)),
            pl.BlockSpec((tf, D), lambda i, j: (j, 0)),
            pl.BlockSpec((1, D), lambda i, j: (0, 0)),
        ],
        out_specs=pl.BlockSpec((tm, D), lambda i, j: (i, 0)),
        out_shape=jax.ShapeDtypeStruct((T, D), F32),
        scratch_shapes=[pltpu.VMEM((tm, D), BF16), pltpu.VMEM((tm, D), F32)],
        compiler_params=_cparams(("parallel", "arbitrary")),
        name="mlp",
    )(x, g1, w1, w2, g2)


def _norm_proj_kernel(x_ref, g_ref, w_ref, o_ref, xn_ref):
    @pl.when(pl.program_id(1) == 0)
    def _():
        xn_ref[...] = (_rms(x_ref[...]) * g_ref[...]).astype(BF16)

    o_ref[...] = _dot(xn_ref[...], w_ref[...]).astype(o_ref.dtype)


def _norm_proj(x, g, w, tm, tn, out_dtype, name):
    T, D = x.shape
    N = w.shape[1]
    return pl.pallas_call(
        _norm_proj_kernel,
        grid=(T // tm, N // tn),
        in_specs=[
            pl.BlockSpec((tm, D), lambda i, j: (i, 0)),
            pl.BlockSpec((1, D), lambda i, j: (0, 0)),
            pl.BlockSpec((D, tn), lambda i, j: (0, j)),
        ],
        out_specs=pl.BlockSpec((tm, tn), lambda i, j: (i, j)),
        out_shape=jax.ShapeDtypeStruct((T, N), out_dtype),
        scratch_shapes=[pltpu.VMEM((tm, D), BF16)],
        compiler_params=_cparams(("parallel", "arbitrary")),
        name=name,
    )(x, g, w)


def _att_inproj_kernel(x_ref, g_ref, wqt_ref, wk_ref, wvt_ref, qn_ref, kn_ref,
                       cqt_ref, sqt_ref, ck_ref, sk_ref, qt_ref, k_ref, vt_ref):
    hd = ATT_HEAD_DIM
    nf = hd // 4
    xn = (_rms(x_ref[...]) * g_ref[...]).astype(BF16)
    tm = xn.shape[0]
    qt = _dot_nt(wqt_ref[...], xn)
    gain = jnp.concatenate([qn_ref[...]] * (tm // LANES), axis=1)
    for hh in range(ATT_Q_HEADS):
        h = qt[hh * hd:(hh + 1) * hd, :]
        h = h * lax.rsqrt(jnp.mean(h * h, axis=0, keepdims=True) + EPS) * gain
        swapped = jnp.concatenate([h[nf:2 * nf], h[0:nf], h[3 * nf:4 * nf], h[2 * nf:3 * nf]], axis=0)
        qt_ref[hh * hd:(hh + 1) * hd, :] = (h * cqt_ref[...] + swapped * sqt_ref[...]).astype(BF16)
    for hh in range(ATT_KV_HEADS):
        h = _dot(xn, wk_ref[:, hh * hd:(hh + 1) * hd])
        h = _rope_piece(_rms(h) * kn_ref[...], ck_ref[...], sk_ref[...], nf)
        k_ref[:, hh * hd:(hh + 1) * hd] = h.astype(BF16)
    vt_ref[...] = _dot_nt(wvt_ref[...], xn).astype(BF16)


def _att_inproj(x, g, wqt, wk, wvt, qn_b, kn, cqt, sqt, ck, sk, S, tm):
    T, D = x.shape
    hd = ATT_HEAD_DIM
    nq, nkv = ATT_Q_HEADS * hd, ATT_KV_HEADS * hd
    nS = S // tm
    const = lambda i: (0, 0)
    return pl.pallas_call(
        _att_inproj_kernel,
        grid=(T // tm,),
        in_specs=[
            pl.BlockSpec((tm, D), lambda i: (i, 0)),
            pl.BlockSpec((1, D), const),
            pl.BlockSpec((nq, D), const),
            pl.BlockSpec((D, nkv), const),
            pl.BlockSpec((nkv, D), const),
            pl.BlockSpec((hd, LANES), const),
            pl.BlockSpec((1, hd), const),
            pl.BlockSpec((hd, tm), lambda i: (0, i % nS)),
            pl.BlockSpec((hd, tm), lambda i: (0, i % nS)),
            pl.BlockSpec((tm, hd), lambda i: (i % nS, 0)),
            pl.BlockSpec((tm, hd), lambda i: (i % nS, 0)),
        ],
        out_specs=[
            pl.BlockSpec((nq, tm), lambda i: (0, i)),
            pl.BlockSpec((tm, nkv), lambda i: (i, 0)),
            pl.BlockSpec((nkv, tm), lambda i: (0, i)),
        ],
        out_shape=[
            jax.ShapeDtypeStruct((nq, T), BF16),
            jax.ShapeDtypeStruct((T, nkv), BF16),
            jax.ShapeDtypeStruct((nkv, T), BF16),
        ],
        compiler_params=_cparams(("parallel",)),
        name="att_inproj",
    )(x, g, wqt, wk, wvt, qn_b, kn, cqt, sqt, ck, sk)


def _flash_kernel(qt_ref, k_ref, vt_ref, o_ref, m_ref, l_ref, acc_ref):
    ki = pl.program_id(3)
    hd = ATT_HEAD_DIM

    @pl.when(ki == 0)
    def _():
        m_ref[...] = jnp.full_like(m_ref, -jnp.inf)
        l_ref[...] = jnp.zeros_like(l_ref)
        acc_ref[...] = jnp.zeros_like(acc_ref)

    k = k_ref[...]
    vt = vt_ref[...]
    st_next = _dot(k, qt_ref[0:hd, :])
    for r in range(ATT_REP):
        st = st_next
        if r + 1 < ATT_REP:
            st_next = _dot(k, qt_ref[(r + 1) * hd:(r + 2) * hd, :])
        m_prev = m_ref[r]
        m_new = jnp.maximum(m_prev, jnp.max(st, axis=0, keepdims=True))
        alpha = jnp.exp2(m_prev - m_new)
        p = jnp.exp2(st - m_new)
        l_ref[r] = alpha * l_ref[r] + jnp.sum(p, axis=0, keepdims=True)
        acc_ref[r] = alpha * acc_ref[r] + _dot(vt, p.astype(BF16))
        m_ref[r] = m_new

    @pl.when(ki == pl.num_programs(3) - 1)
    def _():
        for r in range(ATT_REP):
            o_ref[:, r * hd:(r + 1) * hd] = (acc_ref[r] / l_ref[r]).T.astype(o_ref.dtype)


def _flash(qt, k, vt, B, S, tq, tk):
    T = B * S
    hd = ATT_HEAD_DIM
    nq, nk = S // tq, S // tk
    return pl.pallas_call(
        _flash_kernel,
        grid=(B, ATT_KV_HEADS, nq, nk),
        in_specs=[
            pl.BlockSpec((ATT_REP * hd, tq), lambda b, h, qi, ki: (h, b * nq + qi)),
            pl.BlockSpec((tk, hd), lambda b, h, qi, ki: (b * nk + ki, h)),
            pl.BlockSpec((hd, tk), lambda b, h, qi, ki: (h, b * nk + ki)),
        ],
        out_specs=pl.BlockSpec((tq, ATT_REP * hd), lambda b, h, qi, ki: (b * nq + qi, h)),
        out_shape=jax.ShapeDtypeStruct((T, ATT_Q_HEADS * hd), BF16),
        scratch_shapes=[
            pltpu.VMEM((ATT_REP, 1, tq), F32),
            pltpu.VMEM((ATT_REP, 1, tq), F32),
            pltpu.VMEM((ATT_REP, hd, tq), F32),
        ],
        compiler_params=_cparams(("parallel", "parallel", "parallel", "arbitrary")),
        name="flash",
    )(qt, k, vt)


def _att_out_kernel(o_ref, w_ref, g_ref, x_ref, out_ref):
    out_ref[...] = x_ref[...] + _rms(_dot(o_ref[...], w_ref[...])) * g_ref[...]


def _att_out(o, w, g, x, tm):
    T, D = x.shape
    K = o.shape[1]
    return pl.pallas_call(
        _att_out_kernel,
        grid=(T // tm,),
        in_specs=[
            pl.BlockSpec((tm, K), lambda i: (i, 0)),
            pl.BlockSpec((K, D), lambda i: (0, 0)),
            pl.BlockSpec((1, D), lambda i: (0, 0)),
            pl.BlockSpec((tm, D), lambda i: (i, 0)),
        ],
        out_specs=pl.BlockSpec((tm, D), lambda i: (i, 0)),
        out_shape=jax.ShapeDtypeStruct((T, D), F32),
        compiler_params=_cparams(("parallel",)),
        name="att_out",
    )(o, w, g, x)


def _ssm_xbc_kernel(xp_ref, xc_ref, xn_ref, g_ref, w_ref, cw_ref, cb_ref, o_ref, hn_ref, *, tiles_per_seq):
    i = pl.program_id(0)
    tm = xc_ref.shape[0]

    @pl.when(pl.program_id(1) == 0)
    def _():
        xe = jnp.concatenate([xp_ref[...], xc_ref[...], xn_ref[...]], axis=0)
        hn_ref[...] = (_rms(xe) * g_ref[...]).astype(BF16)

    acc = _dot(hn_ref[...], w_ref[...])
    first = (i % tiles_per_seq) == 0
    last = (i % tiles_per_seq) == tiles_per_seq - 1
    ext = jnp.concatenate([jnp.where(first, 0.0, acc[:SUBLANES]), acc[SUBLANES:SUBLANES + tm],
                           jnp.where(last, 0.0, acc[SUBLANES + tm:])], axis=0)
    n, tn = ext.shape
    ext = ext.reshape(n // SUBLANES, SUBLANES, tn)
    sub = lax.broadcasted_iota(jnp.int32, ext.shape, 1)

    def nxt(z):
        r = pltpu.roll(z, SUBLANES - 1, 1)
        return jnp.where(sub == SUBLANES - 1, jnp.concatenate([r[1:], r[:1]], axis=0), r)

    def prv(z):
        r = pltpu.roll(z, 1, 1)
        return jnp.where(sub == 0, jnp.concatenate([r[-1:], r[:-1]], axis=0), r)

    w = [cw_ref[k:k + 1, :] for k in range(SSM_CONV)]
    y = w[2] * ext + nxt(w[3] * ext + nxt(w[4] * ext)) + prv(w[1] * ext + prv(w[0] * ext))
    y = y[1:1 + tm // SUBLANES].reshape(tm, tn)
    o_ref[...] = _silu(y + cb_ref[...])


def _ssm_xbc(x, g, w, conv_w, conv_b, S, tm, tn):
    T, D = x.shape
    C = w.shape[1]
    rows8 = tm // SUBLANES
    nblk8 = T // SUBLANES
    return pl.pallas_call(
        functools.partial(_ssm_xbc_kernel, tiles_per_seq=S // tm),
        grid=(T // tm, C // tn),
        in_specs=[
            pl.BlockSpec((SUBLANES, D), lambda i, j: (jnp.maximum(i * rows8 - 1, 0), 0)),
            pl.BlockSpec((tm, D), lambda i, j: (i, 0)),
            pl.BlockSpec((SUBLANES, D), lambda i, j: (jnp.minimum((i + 1) * rows8, nblk8 - 1), 0)),
            pl.BlockSpec((1, D), lambda i, j: (0, 0)),
            pl.BlockSpec((D, tn), lambda i, j: (0, j)),
            pl.BlockSpec((SSM_CONV, tn), lambda i, j: (0, j)),
            pl.BlockSpec((1, tn), lambda i, j: (0, j)),
        ],
        out_specs=pl.BlockSpec((tm, tn), lambda i, j: (i, j)),
        out_shape=jax.ShapeDtypeStruct((T, C), F32),
        scratch_shapes=[pltpu.VMEM((tm + 2 * SUBLANES, D), BF16)],
        compiler_params=_cparams(("parallel", "arbitrary")),
        name="ssm_xbc",
    )(x, x, x, g, w, conv_w, conv_b)


def _ssd_scan_kernel(x_ref, b_ref, c_ref, dt_ref, bias_ref, alog_ref, y_ref, st_ref, *, reverse, col0):
    L = SSM_CHUNK
    N = SSM_STATE
    pairs_per_group = SSM_HEADS_PER_GROUP // 2

    @pl.when(pl.program_id(1) == 0)
    def _():
        st_ref[...] = jnp.zeros_like(st_ref)

    log2e = math.log2(math.e)
    dt = _softplus(dt_ref[...] + bias_ref[...])
    a2 = dt * (-jnp.exp(alog_ref[...])) * log2e
    li = lax.broadcasted_iota(jnp.int32, (L, L), 0)
    si = lax.broadcasted_iota(jnp.int32, (L, L), 1)
    tri = (si >= li) if reverse else (si <= li)
    cum = jnp.dot(tri.astype(F32), a2, precision=lax.Precision.HIGHEST, preferred_element_type=F32)
    cum_t = cum.T
    src_t = cum_t - (jnp.log(dt) * log2e).T
    mask = (si > li) if reverse else (si <= li)
    tot = 0 if reverse else L - 1
    low = lax.broadcasted_iota(jnp.int32, (L, LANES), 1) < SSM_HEAD_DIM
    low_row = low[:1]

    c_gs = [c_ref[:, g * N:(g + 1) * N] for g in range(SSM_GROUPS)]
    cbs = [_dot_nt(c_gs[g].astype(BF16), b_ref[:, g * N:(g + 1) * N].astype(BF16))
           for g in range(SSM_GROUPS)]
    b_ts = [b_ref[:, g * N:(g + 1) * N].T for g in range(SSM_GROUPS)]

    for g in range(SSM_GROUPS):
        c_g, cb, b_t = c_gs[g], cbs[g], b_ts[g]
        for jp in range(pairs_per_group):
            pair = g * pairs_per_group + jp
            x16 = x_ref[:, pair * LANES:(pair + 1) * LANES].astype(BF16)
            st = st_ref[pair]
            st16 = st.astype(BF16)
            zero = jnp.zeros_like(x16)
            x_lo, x_hi = jnp.where(low, x16, zero), jnp.where(low, zero, x16)
            s_lo, s_hi = jnp.where(low, st16, zero), jnp.where(low, zero, st16)
            ws, ecs, bts, c_tots = [], [], [], []
            for k in range(2):
                col = col0 + 2 * pair + k
                c_col = jnp.broadcast_to(cum[:, col:col + 1], (L, L))
                src_row = src_t[col:col + 1, :]
                c_tot = cum_t[col:col + 1, tot:tot + 1]
                ws.append(cb * jnp.where(mask, jnp.exp2(c_col - src_row), 0.0))
                ecs.append(c_g * jnp.exp2(c_col))
                bts.append(b_t * jnp.exp2(c_tot - src_row))
                c_tots.append(c_tot)
            lhs = jnp.concatenate(ws + ecs, axis=1).astype(BF16)
            rhs = jnp.concatenate([x_lo, x_hi, s_lo, s_hi], axis=0)
            y_ref[:, pair * LANES:(pair + 1) * LANES] = _dot(lhs, rhs)
            decay = jnp.where(low_row, jnp.exp2(c_tots[0]), jnp.exp2(c_tots[1]))
            upd = _dot(jnp.concatenate(bts, axis=1).astype(BF16), jnp.concatenate([x_lo, x_hi], axis=0))
            st_ref[pair] = st * decay + upd


def _ssd_scan(xbc, zdt, dt_bias, a_log, B, S, reverse):
    T = B * S
    L = SSM_CHUNK
    nC = S // L
    col0 = SSM_HEADS if reverse else 0
    dt_blk = SSM_D_INNER // SSM_DT_PAD

    def row(b, c):
        return b * nC + ((nC - 1 - c) if reverse else c)

    return pl.pallas_call(
        functools.partial(_ssd_scan_kernel, reverse=reverse, col0=col0),
        grid=(B, nC),
        in_specs=[
            pl.BlockSpec((L, SSM_D_INNER), lambda b, c: (row(b, c), 0)),
            pl.BlockSpec((L, SSM_GN), lambda b, c: (row(b, c), SSM_D_INNER // SSM_GN)),
            pl.BlockSpec((L, SSM_GN), lambda b, c: (row(b, c), SSM_D_INNER // SSM_GN + 1)),
            pl.BlockSpec((L, SSM_DT_PAD), lambda b, c: (row(b, c), dt_blk)),
            pl.BlockSpec((1, SSM_DT_PAD), lambda b, c: (0, 0)),
            pl.BlockSpec((1, SSM_DT_PAD), lambda b, c: (0, 0)),
        ],
        out_specs=pl.BlockSpec((L, SSM_D_INNER), lambda b, c: (row(b, c), 0)),
        out_shape=jax.ShapeDtypeStruct((T, SSM_D_INNER), F32),
        scratch_shapes=[pltpu.VMEM((SSM_HEADS // 2, SSM_STATE, 2 * SSM_HEAD_DIM), F32)],
        compiler_params=_cparams(("parallel", "arbitrary")),
        name="ssd_scan_bwd" if reverse else "ssd_scan_fwd",
    )(xbc, xbc, xbc, zdt, dt_bias, a_log)


def _ssm_out_kernel(yf_ref, yb_ref, xs_ref, z_ref, d_ref, nw_ref, w_ref, g_ref, x_ref, out_ref):
    y = yf_ref[...] + yb_ref[...] + xs_ref[...] * d_ref[...]
    y = _rms(y * _silu(z_ref[...])) * nw_ref[...]
    out_ref[...] = x_ref[...] + _rms(_dot(y.astype(BF16), w_ref[...])) * g_ref[...]


def _ssm_out(yf, yb, xbc, zdt, d_exp, norm_w, w_out, g, x, tm):
    T, D = x.shape
    K = SSM_D_INNER
    tok = lambda i: (i, 0)
    const = lambda i: (0, 0)
    return pl.pallas_call(
        _ssm_out_kernel,
        grid=(T // tm,),
        in_specs=[
            pl.BlockSpec((tm, K), tok),
            pl.BlockSpec((tm, K), tok),
            pl.BlockSpec((tm, K), tok),
            pl.BlockSpec((tm, K), tok),
            pl.BlockSpec((1, K), const),
            pl.BlockSpec((1, K), const),
            pl.BlockSpec((K, D), const),
            pl.BlockSpec((1, D), const),
            pl.BlockSpec((tm, D), tok),
        ],
        out_specs=pl.BlockSpec((tm, D), tok),
        out_shape=jax.ShapeDtypeStruct((T, D), F32),
        compiler_params=_cparams(("parallel",)),
        name="ssm_out",
    )(yf, yb, xbc, zdt, d_exp, norm_w, w_out, g, x)


def _ret_qk_kernel(x_ref, g_ref, w_ref, cos_ref, sin_ref, o_ref, xn_ref):
    @pl.when(pl.program_id(1) == 0)
    def _():
        xn_ref[...] = (_rms(x_ref[...]) * g_ref[...]).astype(BF16)

    nf = RET_QK_DIM // 4
    xn = xn_ref[...]
    for p in range(o_ref.shape[1] // LANES):
        c0 = p * LANES
        t0 = (p % (RET_QK_DIM // LANES)) * LANES
        h = _dot(xn, w_ref[:, c0:c0 + LANES])
        h = _rope_piece(h, cos_ref[0, :, t0:t0 + LANES], sin_ref[0, :, t0:t0 + LANES], nf)
        o_ref[:, c0:c0 + LANES] = h.astype(o_ref.dtype)


def _ret_qk(x, g, w, cos, sin, S, tm):
    T, D = x.shape
    nS = S // tm
    return pl.pallas_call(
        _ret_qk_kernel,
        grid=(T // tm, 2),
        in_specs=[
            pl.BlockSpec((tm, D), lambda i, j: (i, 0)),
            pl.BlockSpec((1, D), lambda i, j: (0, 0)),
            pl.BlockSpec((D, D), lambda i, j: (0, j)),
            pl.BlockSpec((1, tm, RET_QK_DIM), lambda i, j: (j, i % nS, 0)),
            pl.BlockSpec((1, tm, RET_QK_DIM), lambda i, j: (j, i % nS, 0)),
        ],
        out_specs=pl.BlockSpec((tm, D), lambda i, j: (i, j)),
        out_shape=jax.ShapeDtypeStruct((T, 2 * D), BF16),
        scratch_shapes=[pltpu.VMEM((tm, D), BF16)],
        compiler_params=_cparams(("parallel", "arbitrary")),
        name="ret_qk",
    )(x, g, w, cos, sin)


def _ret_scan_kernel(qk_ref, v_ref, p_ref, y_ref, r_ref, *, reverse, row0):
    L = RET_CHUNK
    dk, dv = RET_QK_DIM, RET_V_DIM
    rep = dv // LANES

    @pl.when(pl.program_id(1) == 0)
    def _():
        r_ref[...] = jnp.zeros_like(r_ref)

    li = lax.broadcasted_iota(jnp.int32, (L, LANES), 0)
    si = lax.broadcasted_iota(jnp.int32, (L, LANES), 1)
    lf = li.astype(F32)
    if reverse:
        mask = si > li
        diff = (si - li).astype(F32)
        q_pow = L - lf
        k_pow = lf
    else:
        mask = si <= li
        diff = (li - si).astype(F32)
        q_pow = lf + 1.0
        k_pow = L - 1.0 - lf

    heads = range(RET_HEADS)
    qs = [qk_ref[:, h * dk:(h + 1) * dk] for h in heads]
    ks = [qk_ref[:, RET_HEADS * dk + h * dk:RET_HEADS * dk + (h + 1) * dk] for h in heads]
    lgs = [jnp.log1p(-jnp.exp2(p_ref[row0 + h:row0 + h + 1, :])) for h in heads]
    qk = [_dot_nt(qs[h], ks[h]) for h in heads]
    for h in heads:
        lg = lgs[h]
        q_dec = jnp.exp(q_pow * lg)
        k_dec = jnp.exp(k_pow * lg)
        v = v_ref[:, h * dv:(h + 1) * dv]
        r = r_ref[h]
        y_cross = _dot(qs[h], r.astype(BF16)) * jnp.concatenate([q_dec] * rep, axis=1)
        y_ref[:, h * dv:(h + 1) * dv] = y_cross
        v_sc = (v.astype(F32) * jnp.concatenate([k_dec] * rep, axis=1)).astype(BF16)
        r_ref[h] = r * jnp.exp(float(L) * lg[:, :1]) + _dot_tn(ks[h], v_sc)
    for h in heads:
        dmat = jnp.where(mask, jnp.exp(diff * lgs[h]), 0.0)
        scores = (qk[h] * dmat).astype(BF16)
        y_ref[:, h * dv:(h + 1) * dv] += _dot(scores, v_ref[:, h * dv:(h + 1) * dv])


def _ret_scan(qk, v, p_rows, B, S, reverse):
    T = B * S
    L = RET_CHUNK
    nC = S // L
    row0 = RET_HEADS if reverse else 0

    def row(b, c):
        return b * nC + ((nC - 1 - c) if reverse else c)

    return pl.pallas_call(
        functools.partial(_ret_scan_kernel, reverse=reverse, row0=row0),
        grid=(B, nC),
        in_specs=[
            pl.BlockSpec((L, 2 * D_MODEL), lambda b, c: (row(b, c), 0)),
            pl.BlockSpec((L, RET_HEADS * RET_V_DIM), lambda b, c: (row(b, c), 0)),
            pl.BlockSpec((2 * RET_HEADS, LANES), lambda b, c: (0, 0)),
        ],
        out_specs=pl.BlockSpec((L, RET_HEADS * RET_V_DIM), lambda b, c: (row(b, c), 0)),
        out_shape=jax.ShapeDtypeStruct((T, RET_HEADS * RET_V_DIM), F32),
        scratch_shapes=[pltpu.VMEM((RET_HEADS, RET_QK_DIM, RET_V_DIM), F32)],
        compiler_params=_cparams(("parallel", "arbitrary")),
        name="ret_scan_bwd" if reverse else "ret_scan_fwd",
    )(qk, v, p_rows)


def _ret_out_kernel(yf_ref, yb_ref, gate_ref, w_ref, g_ref, x_ref, out_ref):
    dv = RET_V_DIM
    pieces = []
    for h in range(RET_HEADS):
        y = yf_ref[:, h * dv:(h + 1) * dv] + yb_ref[:, h * dv:(h + 1) * dv]
        pieces.append((_silu(gate_ref[:, h * dv:(h + 1) * dv]) * _rms(y)).astype(BF16))
    y = jnp.concatenate(pieces, axis=1)
    out_ref[...] = x_ref[...] + _rms(_dot(y, w_ref[...])) * g_ref[...]


def _ret_out(yf, yb, gate, w_out, g, x, tm):
    T, D = x.shape
    K = RET_HEADS * RET_V_DIM
    tok = lambda i: (i, 0)
    const = lambda i: (0, 0)
    return pl.pallas_call(
        _ret_out_kernel,
        grid=(T // tm,),
        in_specs=[
            pl.BlockSpec((tm, K), tok),
            pl.BlockSpec((tm, K), tok),
            pl.BlockSpec((tm, K), tok),
            pl.BlockSpec((K, D), const),
            pl.BlockSpec((1, D), const),
            pl.BlockSpec((tm, D), tok),
        ],
        out_specs=pl.BlockSpec((tm, D), tok),
        out_shape=jax.ShapeDtypeStruct((T, D), F32),
        compiler_params=_cparams(("parallel",)),
        name="ret_out",
    )(yf, yb, gate, w_out, g, x)


def _rope_tables(S, dim, scale):
    rows = S // GRID_W
    row = jnp.repeat(jnp.arange(rows, dtype=F32), GRID_W)
    col = jnp.tile(jnp.arange(GRID_W, dtype=F32), rows)
    half = dim // 2
    nf = half // 2
    inv = ROPE_THETA ** (-(jnp.arange(nf, dtype=F32) * 2.0) / half)
    ar = row[:, None] * inv
    ac = col[:, None] * inv
    cos = jnp.concatenate([jnp.cos(ar), jnp.cos(ar), jnp.cos(ac), jnp.cos(ac)], axis=1)
    sin = jnp.concatenate([-jnp.sin(ar), jnp.sin(ar), -jnp.sin(ac), jnp.sin(ac)], axis=1)
    return cos * scale, sin * scale


def _row(v):
    return v.reshape(1, -1).astype(F32)


def _tile(n, pref):
    return pref if n % pref == 0 else n


def _ssm_layer(x, B, S, g_pre, g_post, w_in, conv_w, conv_b, dt_bias, a_log, d_skip, norm_w, w_out):
    T = B * S
    w = w_in.astype(BF16)
    g = _row(g_pre)
    xbc_end = SSM_D_INNER + SSM_CONV_DIM
    w_zdt = jnp.concatenate([w[:, :SSM_D_INNER], w[:, xbc_end:]], axis=1)
    w_zdt = jnp.pad(w_zdt, ((0, 0), (0, SSM_ZDT_COLS - w_zdt.shape[1])))
    zdt = _norm_proj(x, g, w_zdt, _tile(T, 512), SSM_ZDT_COLS, F32, "ssm_zdt")
    xbc = _ssm_xbc(x, g, w[:, SSM_D_INNER:xbc_end], conv_w.astype(F32), _row(conv_b), S,
                   _tile(S, 1024), 1024)
    lane_pad = SSM_DT_PAD - 2 * SSM_HEADS
    bias = jnp.pad(dt_bias.reshape(1, -1).astype(F32), ((0, 0), (0, lane_pad)))
    alog = jnp.pad(a_log.reshape(1, -1).astype(F32), ((0, 0), (0, lane_pad)))
    yf = _ssd_scan(xbc, zdt, bias, alog, B, S, reverse=False)
    yb = _ssd_scan(xbc, zdt, bias, alog, B, S, reverse=True)
    d_exp = jnp.repeat(d_skip.astype(F32), SSM_HEAD_DIM).reshape(1, -1)
    return _ssm_out(yf, yb, xbc, zdt, d_exp, _row(norm_w), w_out.astype(BF16), _row(g_post), x,
                    _tile(T, 512))


def _att_layer(x, B, S, g_pre, g_post, w_in, q_norm, k_norm, w_out):
    T = B * S
    hd = ATT_HEAD_DIM
    nq, nkv = ATT_Q_HEADS * hd, ATT_KV_HEADS * hd
    cq, sq = _rope_tables(S, hd, hd ** -0.5 * math.log2(math.e))
    ck, sk = _rope_tables(S, hd, 1.0)
    w = w_in.astype(BF16)
    qn_b = jnp.broadcast_to(q_norm.astype(F32).reshape(hd, 1), (hd, LANES))
    qt, k, vt = _att_inproj(x, _row(g_pre), w[:, :nq].T, w[:, nq:nq + nkv], w[:, nq + nkv:].T,
                            qn_b, _row(k_norm), cq.T, sq.T, ck, sk, S, _tile(S, 512))
    o = _flash(qt, k, vt, B, S, _tile(S, 512), _tile(S, 2048))
    return _att_out(o, w_out.astype(BF16), _row(g_post), x, _tile(T, 1024))


def _ret_layer(x, B, S, g_pre, g_post, w_in, log2_decay, w_out):
    T = B * S
    D = D_MODEL
    cq, sq = _rope_tables(S, RET_QK_DIM, 1.0)
    ck, sk = _rope_tables(S, RET_QK_DIM, RET_QK_DIM ** -0.5)
    cos = jnp.stack([cq, ck])
    sin = jnp.stack([sq, sk])
    w = w_in.astype(BF16)
    g = _row(g_pre)
    qk = _ret_qk(x, g, w[:, :2 * D], cos, sin, S, _tile(S, 1024))
    v = _norm_proj(x, g, w[:, 2 * D:4 * D], _tile(T, 1024), 1024, BF16, "ret_v")
    gate = _norm_proj(x, g, w[:, 4 * D:], _tile(T, 1024), 1024, F32, "ret_gate")
    p_rows = jnp.broadcast_to(log2_decay.reshape(-1, 1).astype(F32), (2 * RET_HEADS, LANES))
    yf = _ret_scan(qk, v, p_rows, B, S, reverse=False)
    yb = _ret_scan(qk, v, p_rows, B, S, reverse=True)
    return _ret_out(yf, yb, gate, w_out.astype(BF16), _row(g_post), x, _tile(T, 512))


def kernel(x, norm_mix_pre, norm_mix_post, norm_ffn_pre, norm_ffn_post, mlp_w_in, mlp_w_out,
           ssm_w_in, ssm_conv_w, ssm_conv_b, ssm_dt_bias, ssm_a_log, ssm_d, ssm_norm, ssm_w_out,
           att_w_in, att_q_norm, att_k_norm, att_w_out,
           ret_w_in, ret_log2_decay, ret_w_out):
    B, S, D = x.shape
    T = B * S
    depth = norm_mix_pre.shape[0]
    h = x.reshape(T, D)
    for i in range(depth):
        kind, j = i % 3, i // 3
        if kind == 0:
            h = _ssm_layer(h, B, S, norm_mix_pre[i], norm_mix_post[i], ssm_w_in[j], ssm_conv_w[j],
                           ssm_conv_b[j], ssm_dt_bias[j], ssm_a_log[j], ssm_d[j], ssm_norm[j],
                           ssm_w_out[j])
        elif kind == 1:
            h = _att_layer(h, B, S, norm_mix_pre[i], norm_mix_post[i], att_w_in[j], att_q_norm[j],
                           att_k_norm[j], att_w_out[j])
        else:
            h = _ret_layer(h, B, S, norm_mix_pre[i], norm_mix_post[i], ret_w_in[j],
                           ret_log2_decay[j], ret_w_out[j])
        h = _mlp(h, _row(norm_ffn_pre[i]), mlp_w_in[i].astype(BF16), mlp_w_out[i].astype(BF16),
                 _row(norm_ffn_post[i]), _tile(T, 1024), 1024)
    return h.reshape(B, S, D)
```

```python
import functools
import math

import jax
import jax.numpy as jnp
from jax import lax
from jax.experimental import pallas as pl
from jax.experimental.pallas import tpu as pltpu

F32 = jnp.float32
BF16 = jnp.bfloat16

EPS = 1e-6
GRID_W = 64
ROPE_THETA = 10000.0

D_MODEL = 1024
D_FF = 4 * D_MODEL

SSM_D_INNER = 2 * D_MODEL
SSM_HEAD_DIM = 64
SSM_HEADS = SSM_D_INNER // SSM_HEAD_DIM
SSM_GROUPS = 8
SSM_HEADS_PER_GROUP = SSM_HEADS // SSM_GROUPS
SSM_STATE = 128
SSM_CONV = 5
SSM_CHUNK = 128
SSM_GN = SSM_GROUPS * SSM_STATE
SSM_CONV_DIM = SSM_D_INNER + 2 * SSM_GN
SSM_XBC_TILE = 8 * SSM_CHUNK
SSM_DT_PAD = 128
SSM_ZDT_COLS = SSM_D_INNER + 2 * SSM_DT_PAD

ATT_HEAD_DIM = 128
ATT_Q_HEADS = D_MODEL // ATT_HEAD_DIM
ATT_KV_HEADS = 2
ATT_REP = ATT_Q_HEADS // ATT_KV_HEADS
ATT_IN_DIM = (ATT_Q_HEADS + 2 * ATT_KV_HEADS) * ATT_HEAD_DIM

RET_HEADS = 4
RET_QK_DIM = D_MODEL // RET_HEADS
RET_V_DIM = 2 * RET_QK_DIM
RET_CHUNK = 128

LANES = 128
SUBLANES = 8
VMEM_LIMIT = 56 * 1024 * 1024


def _cparams(semantics):
    return pltpu.CompilerParams(dimension_semantics=semantics, vmem_limit_bytes=VMEM_LIMIT)


def _rms(x):
    return x * lax.rsqrt(jnp.mean(x * x, axis=-1, keepdims=True) + EPS)


def _silu(x):
    return x * (1.0 / (1.0 + jnp.exp2(x * (-math.log2(math.e)))))


def _softplus(x):
    return jnp.maximum(x, 0.0) + jnp.log1p(jnp.exp(-jnp.abs(x)))


def _dot(a, b):
    return jnp.dot(a, b, preferred_element_type=F32)


def _dot_nt(a, b):
    return lax.dot_general(a, b, (((1,), (1,)), ((), ())), preferred_element_type=F32)


def _dot_tn(a, b):
    return lax.dot_general(a, b, (((0,), (0,)), ((), ())), preferred_element_type=F32)


def _rope_piece(h, cos, sin, nf):
    if 2 * nf == LANES:
        swapped = pltpu.roll(h, nf, 1)
    else:
        lane = lax.broadcasted_iota(jnp.int32, h.shape, 1)
        first = (lane % (2 * nf)) < nf
        swapped = jnp.where(first, pltpu.roll(h, LANES - nf, 1), pltpu.roll(h, nf, 1))
    return h * cos + swapped * sin


def _mlp_kernel(x_ref, g1_ref, w1_ref, w2_ref, g2_ref, o_ref, xn_ref, acc_ref):
    j = pl.program_id(1)

    @pl.when(j == 0)
    def _():
        xn_ref[...] = (_rms(x_ref[...]) * g1_ref[...]).astype(BF16)
        acc_ref[...] = jnp.zeros_like(acc_ref)

    h = jnp.maximum(_dot(xn_ref[...], w1_ref[...]), 0.0)
    acc_ref[...] += _dot((h * h).astype(BF16), w2_ref[...])

    @pl.when(j == pl.num_programs(1) - 1)
    def _():
        o_ref[...] = x_ref[...] + _rms(acc_ref[...]) * g2_ref[...]


def _mlp(x, g1, w1, w2, g2, tm, tf):
    T, D = x.shape
    F = w1.shape[1]
    return pl.pallas_call(
        _mlp_kernel,
        grid=(T // tm, F // tf),
        in_specs=[
            pl.BlockSpec((tm, D), lambda i, j: (i, 0)),
            pl.BlockSpec((1, D), lambda i, j: (0, 0)),
            pl.BlockSpec((D, tf), lambda i, j: (0, j)),
            pl.BlockSpec((tf, D), lambda i, j: (j, 0)),
            pl.BlockSpec((1, D), lambda i, j: (0, 0)),
        ],
        out_specs=pl.BlockSpec((tm, D), lambda i, j: (i, 0)),
        out_shape=jax.ShapeDtypeStruct((T, D), F32),
        scratch_shapes=[pltpu.VMEM((tm, D), BF16), pltpu.VMEM((tm, D), F32)],
        compiler_params=_cparams(("parallel", "arbitrary")),
        name="mlp",
    )(x, g1, w1, w2, g2)


def _norm_proj_kernel(x_ref, g_ref, w_ref, o_ref, xn_ref):
    @pl.when(pl.program_id(1) == 0)
    def _():
        xn_ref[...] = (_rms(x_ref[...]) * g_ref[...]).astype(BF16)

    o_ref[...] = _dot(xn_ref[...], w_ref[...]).astype(o_ref.dtype)


def _norm_proj(x, g, w, tm, tn, out_dtype, name):
    T, D = x.shape
    N = w.shape[1]
    return pl.pallas_call(
        _norm_proj_kernel,
        grid=(T // tm, N // tn),
        in_specs=[
            pl.BlockSpec((tm, D), lambda i, j: (i, 0)),
            pl.BlockSpec((1, D), lambda i, j: (0, 0)),
            pl.BlockSpec((D, tn), lambda i, j: (0, j)),
        ],
        out_specs=pl.BlockSpec((tm, tn), lambda i, j: (i, j)),
        out_shape=jax.ShapeDtypeStruct((T, N), out_dtype),
        scratch_shapes=[pltpu.VMEM((tm, D), BF16)],
        compiler_params=_cparams(("parallel", "arbitrary")),
        name=name,
    )(x, g, w)


def _att_inproj_kernel(x_ref, g_ref, wqt_ref, wk_ref, wvt_ref, qn_ref, kn_ref,
                       cqt_ref, sqt_ref, ck_ref, sk_ref, qt_ref, k_ref, vt_ref):
    hd = ATT_HEAD_DIM
    nf = hd // 4
    xn = (_rms(x_ref[...]) * g_ref[...]).astype(BF16)
    tm = xn.shape[0]
    qt = _dot_nt(wqt_ref[...], xn)
    gain = jnp.concatenate([qn_ref[...]] * (tm // LANES), axis=1)
    for hh in range(ATT_Q_HEADS):
        h = qt[hh * hd:(hh + 1) * hd, :]
        h = h * lax.rsqrt(jnp.mean(h * h, axis=0, keepdims=True) + EPS) * gain
        swapped = jnp.concatenate([h[nf:2 * nf], h[0:nf], h[3 * nf:4 * nf], h[2 * nf:3 * nf]], axis=0)
        qt_ref[hh * hd:(hh + 1) * hd, :] = (h * cqt_ref[...] + swapped * sqt_ref[...]).astype(BF16)
    for hh in range(ATT_KV_HEADS):
        h = _dot(xn, wk_ref[:, hh * hd:(hh + 1) * hd])
        h = _rope_piece(_rms(h) * kn_ref[...], ck_ref[...], sk_ref[...], nf)
        k_ref[:, hh * hd:(hh + 1) * hd] = h.astype(BF16)
    vt_ref[...] = _dot_nt(wvt_ref[...], xn).astype(BF16)


def _att_inproj(x, g, wqt, wk, wvt, qn_b, kn, cqt, sqt, ck, sk, S, tm):
    T, D = x.shape
    hd = ATT_HEAD_DIM
    nq, nkv = ATT_Q_HEADS * hd, ATT_KV_HEADS * hd
    nS = S // tm
    const = lambda i: (0, 0)
    return pl.pallas_call(
        _att_inproj_kernel,
        grid=(T // tm,),
        in_specs=[
            pl.BlockSpec((tm, D), lambda i: (i, 0)),
            pl.BlockSpec((1, D), const),
            pl.BlockSpec((nq, D), const),
            pl.BlockSpec((D, nkv), const),
            pl.BlockSpec((nkv, D), const),
            pl.BlockSpec((hd, LANES), const),
            pl.BlockSpec((1, hd), const),
            pl.BlockSpec((hd, tm), lambda i: (0, i % nS)),
            pl.BlockSpec((hd, tm), lambda i: (0, i % nS)),
            pl.BlockSpec((tm, hd), lambda i: (i % nS, 0)),
            pl.BlockSpec((tm, hd), lambda i: (i % nS, 0)),
        ],
        out_specs=[
            pl.BlockSpec((nq, tm), lambda i: (0, i)),
            pl.BlockSpec((tm, nkv), lambda i: (i, 0)),
            pl.BlockSpec((nkv, tm), lambda i: (0, i)),
        ],
        out_shape=[
            jax.ShapeDtypeStruct((nq, T), BF16),
            jax.ShapeDtypeStruct((T, nkv), BF16),
            jax.ShapeDtypeStruct((nkv, T), BF16),
        ],
        compiler_params=_cparams(("parallel",)),
        name="att_inproj",
    )(x, g, wqt, wk, wvt, qn_b, kn, cqt, sqt, ck, sk)


def _flash_kernel(qt_ref, k_ref, vt_ref, o_ref, m_ref, l_ref, acc_ref):
    ki = pl.program_id(3)
    hd = ATT_HEAD_DIM

    @pl.when(ki == 0)
    def _():
        m_ref[...] = jnp.full_like(m_ref, -jnp.inf)
        l_ref[...] = jnp.zeros_like(l_ref)
        acc_ref[...] = jnp.zeros_like(acc_ref)

    k = k_ref[...]
    vt = vt_ref[...]
    st_next = _dot(k, qt_ref[0:hd, :])
    for r in range(ATT_REP):
        st = st_next
        if r + 1 < ATT_REP:
            st_next = _dot(k, qt_ref[(r + 1) * hd:(r + 2) * hd, :])
        m_prev = m_ref[r]
        m_new = jnp.maximum(m_prev, jnp.max(st, axis=0, keepdims=True))
        alpha = jnp.exp2(m_prev - m_new)
        p = jnp.exp2(st - m_new)
        l_ref[r] = alpha * l_ref[r] + jnp.sum(p, axis=0, keepdims=True)
        acc_ref[r] = alpha * acc_ref[r] + _dot(vt, p.astype(BF16))
        m_ref[r] = m_new

    @pl.when(ki == pl.num_programs(3) - 1)
    def _():
        for r in range(ATT_REP):
            o_ref[:, r * hd:(r + 1) * hd] = (acc_ref[r] / l_ref[r]).T.astype(o_ref.dtype)


def _flash(qt, k, vt, B, S, tq, tk):
    T = B * S
    hd = ATT_HEAD_DIM
    nq, nk = S // tq, S // tk
    return pl.pallas_call(
        _flash_kernel,
        grid=(B, ATT_KV_HEADS, nq, nk),
        in_specs=[
            pl.BlockSpec((ATT_REP * hd, tq), lambda b, h, qi, ki: (h, b * nq + qi)),
            pl.BlockSpec((tk, hd), lambda b, h, qi, ki: (b * nk + ki, h)),
            pl.BlockSpec((hd, tk), lambda b, h, qi, ki: (h, b * nk + ki)),
        ],
        out_specs=pl.BlockSpec((tq, ATT_REP * hd), lambda b, h, qi, ki: (b * nq + qi, h)),
        out_shape=jax.ShapeDtypeStruct((T, ATT_Q_HEADS * hd), BF16),
        scratch_shapes=[
            pltpu.VMEM((ATT_REP, 1, tq), F32),
            pltpu.VMEM((ATT_REP, 1, tq), F32),
            pltpu.VMEM((ATT_REP, hd, tq), F32),
        ],
        compiler_params=_cparams(("parallel", "parallel", "parallel", "arbitrary")),
        name="flash",
    )(qt, k, vt)


def _out_proj_kernel(y_ref, w_ref, g_ref, x_ref, out_ref):
    out_ref[...] = x_ref[...] + _rms(_dot(y_ref[...], w_ref[...])) * g_ref[...]


def _out_proj(y, w, g, x, tm, name):
    T, D = x.shape
    K = y.shape[1]
    return pl.pallas_call(
        _out_proj_kernel,
        grid=(T // tm,),
        in_specs=[
            pl.BlockSpec((tm, K), lambda i: (i, 0)),
            pl.BlockSpec((K, D), lambda i: (0, 0)),
            pl.BlockSpec((1, D), lambda i: (0, 0)),
            pl.BlockSpec((tm, D), lambda i: (i, 0)),
        ],
        out_specs=pl.BlockSpec((tm, D), lambda i: (i, 0)),
        out_shape=jax.ShapeDtypeStruct((T, D), F32),
        compiler_params=_cparams(("parallel",)),
        name=name,
    )(y, w, g, x)


def _ssm_xbc_kernel(xp_ref, xc_ref, xn_ref, g_ref, perm_ref, w_ref, cw_ref, cb_ref, o_ref,
                    hn_ref, hh_ref, *, tiles_per_seq):
    i = pl.program_id(0)
    tm = xc_ref.shape[0]
    J = tm // SUBLANES

    @pl.when(pl.program_id(1) == 0)
    def _():
        hn = (_rms(xc_ref[...]) * g_ref[...]).astype(BF16)
        hn_ref[...] = _dot(perm_ref[...], hn).astype(BF16)
        halo = jnp.concatenate([xp_ref[...], xn_ref[...]], axis=0)
        hh_ref[...] = (_rms(halo) * g_ref[...]).astype(BF16)

    w_tile = w_ref[...]
    tn = w_tile.shape[1]
    z = _dot(hn_ref[...], w_tile).reshape(J, SUBLANES, tn)
    halo = _dot(hh_ref[...], w_tile)
    first = (i % tiles_per_seq) == 0
    last = (i % tiles_per_seq) == tiles_per_seq - 1
    h_prev = jnp.where(first, 0.0, halo[:SUBLANES])
    h_next = jnp.where(last, 0.0, halo[SUBLANES:])
    sub = lax.broadcasted_iota(jnp.int32, (SUBLANES, tn), 0)

    def nxt(u, un):
        edge = jnp.where(sub == SUBLANES - 1, un[0:1], pltpu.roll(u[0], SUBLANES - 1, 0))
        return jnp.concatenate([u[1:], edge[None]], axis=0), pltpu.roll(un, SUBLANES - 1, 0)

    def prv(u, up):
        edge = jnp.where(sub == 0, up[SUBLANES - 1:], pltpu.roll(u[J - 1], 1, 0))
        return jnp.concatenate([edge[None], u[:-1]], axis=0), pltpu.roll(up, 1, 0)

    w = [cw_ref[k:k + 1, :] for k in range(SSM_CONV)]
    a, an = nxt(w[4] * z, w[4] * h_next)
    a, _ = nxt(w[3] * z + a, w[3] * h_next + an)
    b, bp = prv(w[0] * z, w[0] * h_prev)
    b, _ = prv(w[1] * z + b, w[1] * h_prev + bp)
    y = (w[2] * z + a + b + cb_ref[...]).reshape(tm, tn)
    o_ref[...] = _silu(y)


def _ssm_xbc(x, g, w, conv_w, conv_b, S, tm, tn):
    T, D = x.shape
    C = w.shape[1]
    rows8 = tm // SUBLANES
    nblk8 = T // SUBLANES
    p = jnp.arange(tm)
    perm = ((p % SUBLANES) * rows8 + p // SUBLANES)[:, None] == jnp.arange(tm)[None, :]
    return pl.pallas_call(
        functools.partial(_ssm_xbc_kernel, tiles_per_seq=S // tm),
        grid=(T // tm, C // tn),
        in_specs=[
            pl.BlockSpec((SUBLANES, D), lambda i, j: (jnp.maximum(i * rows8 - 1, 0), 0)),
            pl.BlockSpec((tm, D), lambda i, j: (i, 0)),
            pl.BlockSpec((SUBLANES, D), lambda i, j: (jnp.minimum((i + 1) * rows8, nblk8 - 1), 0)),
            pl.BlockSpec((1, D), lambda i, j: (0, 0)),
            pl.BlockSpec((tm, tm), lambda i, j: (0, 0)),
            pl.BlockSpec((D, tn), lambda i, j: (0, j)),
            pl.BlockSpec((SSM_CONV, tn), lambda i, j: (0, j)),
            pl.BlockSpec((1, tn), lambda i, j: (0, j)),
        ],
        out_specs=pl.BlockSpec((tm, tn), lambda i, j: (i, j)),
        out_shape=jax.ShapeDtypeStruct((T, C), F32),
        scratch_shapes=[pltpu.VMEM((tm, D), BF16), pltpu.VMEM((2 * SUBLANES, D), BF16)],
        compiler_params=_cparams(("parallel", "arbitrary")),
        name="ssm_xbc",
    )(x, x, x, g, perm.astype(BF16), w, conv_w, conv_b)


def _ssd_scan_kernel(*refs, reverse, col0, final):
    if final:
        (x_ref, b_ref, c_ref, dt_ref, bias_ref, alog_ref, yo_ref, z_ref, d_ref, nw_ref,
         out_ref, st_ref, y_ref) = refs
    else:
        x_ref, b_ref, c_ref, dt_ref, bias_ref, alog_ref, y_ref, st_ref = refs
    L = SSM_CHUNK
    N = SSM_STATE
    pairs_per_group = SSM_HEADS_PER_GROUP // 2

    @pl.when(pl.program_id(1) == 0)
    def _():
        st_ref[...] = jnp.zeros_like(st_ref)

    log2e = math.log2(math.e)
    dt = _softplus(dt_ref[...] + bias_ref[...])
    a2 = dt * (-jnp.exp(alog_ref[...])) * log2e
    li = lax.broadcasted_iota(jnp.int32, (L, L), 0)
    si = lax.broadcasted_iota(jnp.int32, (L, L), 1)
    tri = (si >= li) if reverse else (si <= li)
    cum = jnp.dot(tri.astype(F32), a2, precision=lax.Precision.HIGHEST, preferred_element_type=F32)
    cum_t = cum.T
    src_t = cum_t - (jnp.log(dt) * log2e).T
    mask = (si > li) if reverse else (si <= li)
    tot = 0 if reverse else L - 1
    low = lax.broadcasted_iota(jnp.int32, (L, LANES), 1) < SSM_HEAD_DIM
    low_row = low[:1]

    c_gs = [c_ref[:, g * N:(g + 1) * N] for g in range(SSM_GROUPS)]
    cbs = [_dot_nt(c_gs[g].astype(BF16), b_ref[:, g * N:(g + 1) * N].astype(BF16))
           for g in range(SSM_GROUPS)]
    b_ts = [b_ref[:, g * N:(g + 1) * N].T for g in range(SSM_GROUPS)]

    for g in range(SSM_GROUPS):
        c_g, cb, b_t = c_gs[g], cbs[g], b_ts[g]
        for jp in range(pairs_per_group):
            pair = g * pairs_per_group + jp
            x16 = x_ref[:, pair * LANES:(pair + 1) * LANES].astype(BF16)
            st = st_ref[pair]
            st16 = st.astype(BF16)
            zero = jnp.zeros_like(x16)
            x_lo, x_hi = jnp.where(low, x16, zero), jnp.where(low, zero, x16)
            s_lo, s_hi = jnp.where(low, st16, zero), jnp.where(low, zero, st16)
            ws, ecs, bts, c_tots = [], [], [], []
            for k in range(2):
                col = col0 + 2 * pair + k
                c_col = jnp.broadcast_to(cum[:, col:col + 1], (L, L))
                src_row = src_t[col:col + 1, :]
                c_tot = cum_t[col:col + 1, tot:tot + 1]
                ws.append(cb * jnp.where(mask, jnp.exp2(c_col - src_row), 0.0))
                ecs.append(c_g * jnp.exp2(c_col))
                bts.append(b_t * jnp.exp2(c_tot - src_row))
                c_tots.append(c_tot)
            lhs = jnp.concatenate(ws + ecs, axis=1).astype(BF16)
            rhs = jnp.concatenate([x_lo, x_hi, s_lo, s_hi], axis=0)
            cols = slice(pair * LANES, (pair + 1) * LANES)
            y = _dot(lhs, rhs)
            if final:
                y = (yo_ref[:, cols] + y + x_ref[:, cols] * d_ref[:, cols]) * _silu(z_ref[:, cols])
                sumsq = y * y if pair == 0 else sumsq + y * y
            y_ref[:, cols] = y
            decay = jnp.where(low_row, jnp.exp2(c_tots[0]), jnp.exp2(c_tots[1]))
            upd = _dot(jnp.concatenate(bts, axis=1).astype(BF16), jnp.concatenate([x_lo, x_hi], axis=0))
            st_ref[pair] = st * decay + upd

    if final:
        scale = lax.rsqrt(jnp.sum(sumsq, axis=1, keepdims=True) * (1.0 / SSM_D_INNER) + EPS)
        out_ref[...] = (y_ref[...] * scale * nw_ref[...]).astype(BF16)


def _ssd_scan(xbc, zdt, dt_bias, a_log, B, S, reverse, tail=None):
    T = B * S
    L = SSM_CHUNK
    nC = S // L
    K = SSM_D_INNER
    col0 = SSM_HEADS if reverse else 0
    dt_blk = K // SSM_DT_PAD
    final = tail is not None

    def row(b, c):
        return b * nC + ((nC - 1 - c) if reverse else c)

    chunk = lambda b, c: (row(b, c), 0)
    const = lambda b, c: (0, 0)
    per_tile = SSM_XBC_TILE // L
    xbc2 = xbc.reshape(T // per_tile, per_tile * SSM_CONV_DIM)
    band = lambda b, c, width, off: (row(b, c) // per_tile,
                                     (row(b, c) % per_tile) * (SSM_CONV_DIM // width) + off)
    in_specs = [
        pl.BlockSpec((L, K), lambda b, c: band(b, c, K, 0)),
        pl.BlockSpec((L, SSM_GN), lambda b, c: band(b, c, SSM_GN, K // SSM_GN)),
        pl.BlockSpec((L, SSM_GN), lambda b, c: band(b, c, SSM_GN, K // SSM_GN + 1)),
        pl.BlockSpec((L, SSM_DT_PAD), lambda b, c: (row(b, c), dt_blk)),
        pl.BlockSpec((1, SSM_DT_PAD), const),
        pl.BlockSpec((1, SSM_DT_PAD), const),
    ]
    operands = [xbc2, xbc2, xbc2, zdt, dt_bias, a_log]
    scratch = [pltpu.VMEM((SSM_HEADS // 2, SSM_STATE, 2 * SSM_HEAD_DIM), F32)]
    if final:
        y_other, d_exp, norm_w = tail
        in_specs += [pl.BlockSpec((L, K), chunk), pl.BlockSpec((L, K), chunk),
                     pl.BlockSpec((1, K), const), pl.BlockSpec((1, K), const)]
        operands += [y_other, zdt, d_exp, norm_w]
        scratch.append(pltpu.VMEM((L, K), F32))
    return pl.pallas_call(
        functools.partial(_ssd_scan_kernel, reverse=reverse, col0=col0, final=final),
        grid=(B, nC),
        in_specs=in_specs,
        out_specs=pl.BlockSpec((L, K), chunk),
        out_shape=jax.ShapeDtypeStruct((T, K), BF16 if final else F32),
        scratch_shapes=scratch,
        compiler_params=_cparams(("parallel", "arbitrary")),
        name="ssd_scan_bwd" if reverse else "ssd_scan_fwd",
    )(*operands)


def _ret_qk_kernel(x_ref, g_ref, w_ref, cos_ref, sin_ref, o_ref, xn_ref):
    @pl.when(pl.program_id(1) == 0)
    def _():
        xn_ref[...] = (_rms(x_ref[...]) * g_ref[...]).astype(BF16)

    nf = RET_QK_DIM // 4
    acc = _dot(xn_ref[...], w_ref[...])
    for p in range(o_ref.shape[1] // LANES):
        c0 = p * LANES
        t0 = (p % (RET_QK_DIM // LANES)) * LANES
        h = _rope_piece(acc[:, c0:c0 + LANES], cos_ref[0, :, t0:t0 + LANES],
                        sin_ref[0, :, t0:t0 + LANES], nf)
        o_ref[:, c0:c0 + LANES] = h.astype(o_ref.dtype)


def _ret_qk(x, g, w, cos, sin, S, tm):
    T, D = x.shape
    nS = S // tm
    return pl.pallas_call(
        _ret_qk_kernel,
        grid=(T // tm, 2),
        in_specs=[
            pl.BlockSpec((tm, D), lambda i, j: (i, 0)),
            pl.BlockSpec((1, D), lambda i, j: (0, 0)),
            pl.BlockSpec((D, D), lambda i, j: (0, j)),
            pl.BlockSpec((1, tm, RET_QK_DIM), lambda i, j: (j, i % nS, 0)),
            pl.BlockSpec((1, tm, RET_QK_DIM), lambda i, j: (j, i % nS, 0)),
        ],
        out_specs=pl.BlockSpec((tm, D), lambda i, j: (i, j)),
        out_shape=jax.ShapeDtypeStruct((T, 2 * D), BF16),
        scratch_shapes=[pltpu.VMEM((tm, D), BF16)],
        compiler_params=_cparams(("parallel", "arbitrary")),
        name="ret_qk",
    )(x, g, w, cos, sin)


def _ret_scan_kernel(*refs, reverse, row0, final):
    if final:
        qk_ref, v_ref, p_ref, yo_ref, gate_ref, out_ref, r_ref, y_ref = refs
    else:
        qk_ref, v_ref, p_ref, y_ref, r_ref = refs
    L = RET_CHUNK
    dk, dv = RET_QK_DIM, RET_V_DIM
    rep = dv // LANES

    @pl.when(pl.program_id(1) == 0)
    def _():
        r_ref[...] = jnp.zeros_like(r_ref)

    li = lax.broadcasted_iota(jnp.int32, (L, LANES), 0)
    si = lax.broadcasted_iota(jnp.int32, (L, LANES), 1)
    lf = li.astype(F32)
    if reverse:
        mask = si > li
        diff = (si - li).astype(F32)
        q_pow = L - lf
        k_pow = lf
    else:
        mask = si <= li
        diff = (li - si).astype(F32)
        q_pow = lf + 1.0
        k_pow = L - 1.0 - lf

    heads = range(RET_HEADS)
    qs = [qk_ref[:, h * dk:(h + 1) * dk] for h in heads]
    ks = [qk_ref[:, RET_HEADS * dk + h * dk:RET_HEADS * dk + (h + 1) * dk] for h in heads]
    lgs = [jnp.log1p(-jnp.exp2(p_ref[row0 + h:row0 + h + 1, :])) for h in heads]
    qk = [_dot_nt(qs[h], ks[h]) for h in heads]
    for h in heads:
        lg = lgs[h]
        q_dec = jnp.exp(q_pow * lg)
        k_dec = jnp.exp(k_pow * lg)
        v = v_ref[:, h * dv:(h + 1) * dv]
        r = r_ref[h]
        y_cross = _dot(qs[h], r.astype(BF16)) * jnp.concatenate([q_dec] * rep, axis=1)
        y_ref[:, h * dv:(h + 1) * dv] = y_cross
        v_sc = (v.astype(F32) * jnp.concatenate([k_dec] * rep, axis=1)).astype(BF16)
        r_ref[h] = r * jnp.exp(float(L) * lg[:, :1]) + _dot_tn(ks[h], v_sc)
    for h in heads:
        dmat = jnp.where(mask, jnp.exp(diff * lgs[h]), 0.0)
        scores = (qk[h] * dmat).astype(BF16)
        cols = slice(h * dv, (h + 1) * dv)
        y = y_ref[:, cols] + _dot(scores, v_ref[:, cols])
        if final:
            out_ref[:, cols] = (_silu(gate_ref[:, cols]) * _rms(yo_ref[:, cols] + y)).astype(BF16)
        else:
            y_ref[:, cols] = y


def _ret_scan(qk, v, p_rows, B, S, reverse, tail=None):
    T = B * S
    L = RET_CHUNK
    nC = S // L
    K = RET_HEADS * RET_V_DIM
    row0 = RET_HEADS if reverse else 0
    final = tail is not None

    def row(b, c):
        return b * nC + ((nC - 1 - c) if reverse else c)

    chunk = lambda b, c: (row(b, c), 0)
    in_specs = [
        pl.BlockSpec((L, 2 * D_MODEL), chunk),
        pl.BlockSpec((L, K), chunk),
        pl.BlockSpec((2 * RET_HEADS, LANES), lambda b, c: (0, 0)),
    ]
    operands = [qk, v, p_rows]
    scratch = [pltpu.VMEM((RET_HEADS, RET_QK_DIM, RET_V_DIM), F32)]
    if final:
        in_specs += [pl.BlockSpec((L, K), chunk), pl.BlockSpec((L, K), chunk)]
        operands += list(tail)
        scratch.append(pltpu.VMEM((L, K), F32))
    return pl.pallas_call(
        functools.partial(_ret_scan_kernel, reverse=reverse, row0=row0, final=final),
        grid=(B, nC),
        in_specs=in_specs,
        out_specs=pl.BlockSpec((L, K), chunk),
        out_shape=jax.ShapeDtypeStruct((T, K), BF16 if final else F32),
        scratch_shapes=scratch,
        compiler_params=_cparams(("parallel", "arbitrary")),
        name="ret_scan_bwd" if reverse else "ret_scan_fwd",
    )(*operands)


def _rope_tables(S, dim, scale):
    rows = S // GRID_W
    row = jnp.repeat(jnp.arange(rows, dtype=F32), GRID_W)
    col = jnp.tile(jnp.arange(GRID_W, dtype=F32), rows)
    half = dim // 2
    nf = half // 2
    inv = ROPE_THETA ** (-(jnp.arange(nf, dtype=F32) * 2.0) / half)
    ar = row[:, None] * inv
    ac = col[:, None] * inv
    cos = jnp.concatenate([jnp.cos(ar), jnp.cos(ar), jnp.cos(ac), jnp.cos(ac)], axis=1)
    sin = jnp.concatenate([-jnp.sin(ar), jnp.sin(ar), -jnp.sin(ac), jnp.sin(ac)], axis=1)
    return cos * scale, sin * scale


def _row(v):
    return v.reshape(1, -1).astype(F32)


def _tile(n, pref):
    return pref if n % pref == 0 else n


def _ssm_layer(x, B, S, g_pre, g_post, w_in, conv_w, conv_b, dt_bias, a_log, d_skip, norm_w, w_out):
    T = B * S
    w = w_in.astype(BF16)
    g = _row(g_pre)
    xbc_end = SSM_D_INNER + SSM_CONV_DIM
    w_zdt = jnp.concatenate([w[:, :SSM_D_INNER], w[:, xbc_end:]], axis=1)
    w_zdt = jnp.pad(w_zdt, ((0, 0), (0, SSM_ZDT_COLS - w_zdt.shape[1])))
    zdt = _norm_proj(x, g, w_zdt, _tile(T, 512), SSM_ZDT_COLS, F32, "ssm_zdt")
    assert S % SSM_XBC_TILE == 0, "sequence length must be a multiple of the xBC token tile"
    xbc = _ssm_xbc(x, g, w[:, SSM_D_INNER:xbc_end], conv_w.astype(F32), _row(conv_b), S,
                   SSM_XBC_TILE, 1024)
    lane_pad = SSM_DT_PAD - 2 * SSM_HEADS
    bias = jnp.pad(dt_bias.reshape(1, -1).astype(F32), ((0, 0), (0, lane_pad)))
    alog = jnp.pad(a_log.reshape(1, -1).astype(F32), ((0, 0), (0, lane_pad)))
    d_exp = jnp.repeat(d_skip.astype(F32), SSM_HEAD_DIM).reshape(1, -1)
    yf = _ssd_scan(xbc, zdt, bias, alog, B, S, reverse=False)
    y = _ssd_scan(xbc, zdt, bias, alog, B, S, reverse=True, tail=(yf, d_exp, _row(norm_w)))
    return _out_proj(y, w_out.astype(BF16), _row(g_post), x, _tile(T, 1024), "ssm_out")


def _att_layer(x, B, S, g_pre, g_post, w_in, q_norm, k_norm, w_out):
    T = B * S
    hd = ATT_HEAD_DIM
    nq, nkv = ATT_Q_HEADS * hd, ATT_KV_HEADS * hd
    cq, sq = _rope_tables(S, hd, hd ** -0.5 * math.log2(math.e))
    ck, sk = _rope_tables(S, hd, 1.0)
    w = w_in.astype(BF16)
    qn_b = jnp.broadcast_to(q_norm.astype(F32).reshape(hd, 1), (hd, LANES))
    qt, k, vt = _att_inproj(x, _row(g_pre), w[:, :nq].T, w[:, nq:nq + nkv], w[:, nq + nkv:].T,
                            qn_b, _row(k_norm), cq.T, sq.T, ck, sk, S, _tile(S, 512))
    o = _flash(qt, k, vt, B, S, _tile(S, 512), _tile(S, 2048))
    return _out_proj(o, w_out.astype(BF16), _row(g_post), x, _tile(T, 1024), "att_out")


def _ret_layer(x, B, S, g_pre, g_post, w_in, log2_decay, w_out):
    T = B * S
    D = D_MODEL
    cq, sq = _rope_tables(S, RET_QK_DIM, 1.0)
    ck, sk = _rope_tables(S, RET_QK_DIM, RET_QK_DIM ** -0.5)
    cos = jnp.stack([cq, ck])
    sin = jnp.stack([sq, sk])
    w = w_in.astype(BF16)
    g = _row(g_pre)
    qk = _ret_qk(x, g, w[:, :2 * D], cos, sin, S, _tile(S, 1024))
    v = _norm_proj(x, g, w[:, 2 * D:4 * D], _tile(T, 1024), 1024, BF16, "ret_v")
    gate = _norm_proj(x, g, w[:, 4 * D:], _tile(T, 1024), 1024, F32, "ret_gate")
    p_rows = jnp.broadcast_to(log2_decay.reshape(-1, 1).astype(F32), (2 * RET_HEADS, LANES))
    yf = _ret_scan(qk, v, p_rows, B, S, reverse=False)
    y = _ret_scan(qk, v, p_rows, B, S, reverse=True, tail=(yf, gate))
    return _out_proj(y, w_out.astype(BF16), _row(g_post), x, _tile(T, 1024), "ret_out")


def kernel(x, norm_mix_pre, norm_mix_post, norm_ffn_pre, norm_ffn_post, mlp_w_in, mlp_w_out,
           ssm_w_in, ssm_conv_w, ssm_conv_b, ssm_dt_bias, ssm_a_log, ssm_d, ssm_norm, ssm_w_out,
           att_w_in, att_q_norm, att_k_norm, att_w_out,
           ret_w_in, ret_log2_decay, ret_w_out):
    B, S, D = x.shape
    T = B * S
    depth = norm_mix_pre.shape[0]
    h = x.reshape(T, D)
    for i in range(depth):
        kind, j = i % 3, i // 3
        if kind == 0:
            h = _ssm_layer(h, B, S, norm_mix_pre[i], norm_mix_post[i], ssm_w_in[j], ssm_conv_w[j],
                           ssm_conv_b[j], ssm_dt_bias[j], ssm_a_log[j], ssm_d[j], ssm_norm[j],
                           ssm_w_out[j])
        elif kind == 1:
            h = _att_layer(h, B, S, norm_mix_pre[i], norm_mix_post[i], att_w_in[j], att_q_norm[j],
                           att_k_norm[j], att_w_out[j])
        else:
            h = _ret_layer(h, B, S, norm_mix_pre[i], norm_mix_post[i], ret_w_in[j],
                           ret_log2_decay[j], ret_w_out[j])
        h = _mlp(h, _row(norm_ffn_pre[i]), mlp_w_in[i].astype(BF16), mlp_w_out[i].astype(BF16),
                 _row(norm_ffn_post[i]), _tile(T, 1024), 1024)
    return h.reshape(B, S, D)
```

```python
import functools
import math

import jax
import jax.numpy as jnp
from jax import lax
from jax.experimental import pallas as pl
from jax.experimental.pallas import tpu as pltpu

F32 = jnp.float32
BF16 = jnp.bfloat16

EPS = 1e-6
GRID_W = 64
ROPE_THETA = 10000.0

D_MODEL = 1024
D_FF = 4 * D_MODEL

SSM_D_INNER = 2 * D_MODEL
SSM_HEAD_DIM = 64
SSM_HEADS = SSM_D_INNER // SSM_HEAD_DIM
SSM_GROUPS = 8
SSM_HEADS_PER_GROUP = SSM_HEADS // SSM_GROUPS
SSM_STATE = 128
SSM_CONV = 5
SSM_CHUNK = 128
SSM_GN = SSM_GROUPS * SSM_STATE
SSM_CONV_DIM = SSM_D_INNER + 2 * SSM_GN
SSM_DT_PAD = 128
SSM_ZDT_COLS = SSM_D_INNER + 2 * SSM_DT_PAD

ATT_HEAD_DIM = 128
ATT_Q_HEADS = D_MODEL // ATT_HEAD_DIM
ATT_KV_HEADS = 2
ATT_REP = ATT_Q_HEADS // ATT_KV_HEADS
ATT_IN_DIM = (ATT_Q_HEADS + 2 * ATT_KV_HEADS) * ATT_HEAD_DIM

RET_HEADS = 4
RET_QK_DIM = D_MODEL // RET_HEADS
RET_V_DIM = 2 * RET_QK_DIM
RET_CHUNK = 128

LANES = 128
SUBLANES = 8
VMEM_LIMIT = 56 * 1024 * 1024


def _cparams(semantics):
    return pltpu.CompilerParams(dimension_semantics=semantics, vmem_limit_bytes=VMEM_LIMIT)


def _rms(x):
    return x * lax.rsqrt(jnp.mean(x * x, axis=-1, keepdims=True) + EPS)


def _silu(x):
    return x * (1.0 / (1.0 + jnp.exp2(x * (-math.log2(math.e)))))


def _softplus(x):
    return jnp.maximum(x, 0.0) + jnp.log1p(jnp.exp(-jnp.abs(x)))


def _dot(a, b):
    return jnp.dot(a, b, preferred_element_type=F32)


def _dot_nt(a, b):
    return lax.dot_general(a, b, (((1,), (1,)), ((), ())), preferred_element_type=F32)


def _dot_tn(a, b):
    return lax.dot_general(a, b, (((0,), (0,)), ((), ())), preferred_element_type=F32)


def _rope_piece(h, cos, sin, nf):
    if 2 * nf == LANES:
        swapped = pltpu.roll(h, nf, 1)
    else:
        lane = lax.broadcasted_iota(jnp.int32, h.shape, 1)
        first = (lane % (2 * nf)) < nf
        swapped = jnp.where(first, pltpu.roll(h, LANES - nf, 1), pltpu.roll(h, nf, 1))
    return h * cos + swapped * sin


def _mlp_kernel(x_ref, g1_ref, w1_ref, w2_ref, g2_ref, o_ref, xn_ref, acc_ref):
    j = pl.program_id(1)

    @pl.when(j == 0)
    def _():
        xn_ref[...] = (_rms(x_ref[...]) * g1_ref[...]).astype(BF16)
        acc_ref[...] = jnp.zeros_like(acc_ref)

    h = jnp.maximum(_dot(xn_ref[...], w1_ref[...]), 0.0)
    acc_ref[...] += _dot((h * h).astype(BF16), w2_ref[...])

    @pl.when(j == pl.num_programs(1) - 1)
    def _():
        o_ref[...] = x_ref[...] + _rms(acc_ref[...]) * g2_ref[...]


def _mlp(x, g1, w1, w2, g2, layer, tm, tf):
    T, D = x.shape
    F = w1.shape[2]
    return pl.pallas_call(
        _mlp_kernel,
        grid=(T // tm, F // tf),
        in_specs=[
            pl.BlockSpec((tm, D), lambda i, j: (i, 0)),
            pl.BlockSpec((1, D), lambda i, j: (0, 0)),
            pl.BlockSpec((None, D, tf), lambda i, j: (layer, 0, j)),
            pl.BlockSpec((None, tf, D), lambda i, j: (layer, j, 0)),
            pl.BlockSpec((1, D), lambda i, j: (0, 0)),
        ],
        out_specs=pl.BlockSpec((tm, D), lambda i, j: (i, 0)),
        out_shape=jax.ShapeDtypeStruct((T, D), F32),
        scratch_shapes=[pltpu.VMEM((tm, D), BF16), pltpu.VMEM((tm, D), F32)],
        compiler_params=_cparams(("parallel", "arbitrary")),
        name="mlp",
    )(x, g1, w1, w2, g2)


def _norm_proj_kernel(x_ref, g_ref, w_ref, o_ref, xn_ref):
    @pl.when(pl.program_id(1) == 0)
    def _():
        xn_ref[...] = (_rms(x_ref[...]) * g_ref[...]).astype(BF16)

    o_ref[...] = _dot(xn_ref[...], w_ref[...]).astype(o_ref.dtype)


def _norm_proj(x, g, w, tm, tn, out_dtype, name):
    T, D = x.shape
    N = w.shape[1]
    return pl.pallas_call(
        _norm_proj_kernel,
        grid=(T // tm, N // tn),
        in_specs=[
            pl.BlockSpec((tm, D), lambda i, j: (i, 0)),
            pl.BlockSpec((1, D), lambda i, j: (0, 0)),
            pl.BlockSpec((D, tn), lambda i, j: (0, j)),
        ],
        out_specs=pl.BlockSpec((tm, tn), lambda i, j: (i, j)),
        out_shape=jax.ShapeDtypeStruct((T, N), out_dtype),
        scratch_shapes=[pltpu.VMEM((tm, D), BF16)],
        compiler_params=_cparams(("parallel", "arbitrary")),
        name=name,
    )(x, g, w)


def _att_inproj_kernel(x_ref, g_ref, wqt_ref, wk_ref, wvt_ref, qn_ref, kn_ref,
                       cqt_ref, sqt_ref, ck_ref, sk_ref, qt_ref, k_ref, vt_ref):
    hd = ATT_HEAD_DIM
    nf = hd // 4
    xn = (_rms(x_ref[...]) * g_ref[...]).astype(BF16)
    tm = xn.shape[0]
    qt = _dot_nt(wqt_ref[...], xn)
    gain = jnp.concatenate([qn_ref[...]] * (tm // LANES), axis=1)
    for hh in range(ATT_Q_HEADS):
        h = qt[hh * hd:(hh + 1) * hd, :]
        h = h * lax.rsqrt(jnp.mean(h * h, axis=0, keepdims=True) + EPS) * gain
        swapped = jnp.concatenate([h[nf:2 * nf], h[0:nf], h[3 * nf:4 * nf], h[2 * nf:3 * nf]], axis=0)
        qt_ref[hh * hd:(hh + 1) * hd, :] = (h * cqt_ref[...] + swapped * sqt_ref[...]).astype(BF16)
    for hh in range(ATT_KV_HEADS):
        h = _dot(xn, wk_ref[:, hh * hd:(hh + 1) * hd])
        h = _rope_piece(_rms(h) * kn_ref[...], ck_ref[...], sk_ref[...], nf)
        k_ref[:, hh * hd:(hh + 1) * hd] = h.astype(BF16)
    vt_ref[...] = _dot_nt(wvt_ref[...], xn).astype(BF16)


def _att_inproj(x, g, wqt, wk, wvt, qn_b, kn, cqt, sqt, ck, sk, S, tm):
    T, D = x.shape
    hd = ATT_HEAD_DIM
    nq, nkv = ATT_Q_HEADS * hd, ATT_KV_HEADS * hd
    nS = S // tm
    const = lambda i: (0, 0)
    return pl.pallas_call(
        _att_inproj_kernel,
        grid=(T // tm,),
        in_specs=[
            pl.BlockSpec((tm, D), lambda i: (i, 0)),
            pl.BlockSpec((1, D), const),
            pl.BlockSpec((nq, D), const),
            pl.BlockSpec((D, nkv), const),
            pl.BlockSpec((nkv, D), const),
            pl.BlockSpec((hd, LANES), const),
            pl.BlockSpec((1, hd), const),
            pl.BlockSpec((hd, tm), lambda i: (0, i % nS)),
            pl.BlockSpec((hd, tm), lambda i: (0, i % nS)),
            pl.BlockSpec((tm, hd), lambda i: (i % nS, 0)),
            pl.BlockSpec((tm, hd), lambda i: (i % nS, 0)),
        ],
        out_specs=[
            pl.BlockSpec((nq, tm), lambda i: (0, i)),
            pl.BlockSpec((tm, nkv), lambda i: (i, 0)),
            pl.BlockSpec((nkv, tm), lambda i: (0, i)),
        ],
        out_shape=[
            jax.ShapeDtypeStruct((nq, T), BF16),
            jax.ShapeDtypeStruct((T, nkv), BF16),
            jax.ShapeDtypeStruct((nkv, T), BF16),
        ],
        compiler_params=_cparams(("parallel",)),
        name="att_inproj",
    )(x, g, wqt, wk, wvt, qn_b, kn, cqt, sqt, ck, sk)


def _flash_kernel(qt_ref, k_ref, vt_ref, o_ref, m_ref, l_ref, acc_ref):
    ki = pl.program_id(3)
    hd = ATT_HEAD_DIM

    @pl.when(ki == 0)
    def _():
        m_ref[...] = jnp.full_like(m_ref, -jnp.inf)
        l_ref[...] = jnp.zeros_like(l_ref)
        acc_ref[...] = jnp.zeros_like(acc_ref)

    k = k_ref[...]
    vt = vt_ref[...]
    st_next = _dot(k, qt_ref[0:hd, :])
    for r in range(ATT_REP):
        st = st_next
        if r + 1 < ATT_REP:
            st_next = _dot(k, qt_ref[(r + 1) * hd:(r + 2) * hd, :])
        m_prev = m_ref[r]
        m_new = jnp.maximum(m_prev, jnp.max(st, axis=0, keepdims=True))
        alpha = jnp.exp2(m_prev - m_new)
        p = jnp.exp2(st - m_new)
        l_ref[r] = alpha * l_ref[r] + jnp.sum(p, axis=0, keepdims=True)
        acc_ref[r] = alpha * acc_ref[r] + _dot(vt, p.astype(BF16))
        m_ref[r] = m_new

    @pl.when(ki == pl.num_programs(3) - 1)
    def _():
        for r in range(ATT_REP):
            o_ref[:, r * hd:(r + 1) * hd] = (acc_ref[r] / l_ref[r]).T.astype(o_ref.dtype)


def _flash(qt, k, vt, B, S, tq, tk):
    T = B * S
    hd = ATT_HEAD_DIM
    nq, nk = S // tq, S // tk
    return pl.pallas_call(
        _flash_kernel,
        grid=(B, ATT_KV_HEADS, nq, nk),
        in_specs=[
            pl.BlockSpec((ATT_REP * hd, tq), lambda b, h, qi, ki: (h, b * nq + qi)),
            pl.BlockSpec((tk, hd), lambda b, h, qi, ki: (b * nk + ki, h)),
            pl.BlockSpec((hd, tk), lambda b, h, qi, ki: (h, b * nk + ki)),
        ],
        out_specs=pl.BlockSpec((tq, ATT_REP * hd), lambda b, h, qi, ki: (b * nq + qi, h)),
        out_shape=jax.ShapeDtypeStruct((T, ATT_Q_HEADS * hd), BF16),
        scratch_shapes=[
            pltpu.VMEM((ATT_REP, 1, tq), F32),
            pltpu.VMEM((ATT_REP, 1, tq), F32),
            pltpu.VMEM((ATT_REP, hd, tq), F32),
        ],
        compiler_params=_cparams(("parallel", "parallel", "parallel", "arbitrary")),
        name="flash",
    )(qt, k, vt)


def _out_proj_kernel(y_ref, w_ref, g_ref, x_ref, out_ref):
    out_ref[...] = x_ref[...] + _rms(_dot(y_ref[...], w_ref[...])) * g_ref[...]


def _out_proj(y, w, g, x, tm, name):
    T, D = x.shape
    K = y.shape[1]
    return pl.pallas_call(
        _out_proj_kernel,
        grid=(T // tm,),
        in_specs=[
            pl.BlockSpec((tm, K), lambda i: (i, 0)),
            pl.BlockSpec((K, D), lambda i: (0, 0)),
            pl.BlockSpec((1, D), lambda i: (0, 0)),
            pl.BlockSpec((tm, D), lambda i: (i, 0)),
        ],
        out_specs=pl.BlockSpec((tm, D), lambda i: (i, 0)),
        out_shape=jax.ShapeDtypeStruct((T, D), F32),
        compiler_params=_cparams(("parallel",)),
        name=name,
    )(y, w, g, x)


def _ssm_xbc_kernel(xp_ref, xc_ref, xn_ref, g_ref, w_ref, cw_ref, cb_ref, o_ref, hn_ref, *, tiles_per_seq):
    i = pl.program_id(0)
    tm = xc_ref.shape[0]

    @pl.when(pl.program_id(1) == 0)
    def _():
        xe = jnp.concatenate([xp_ref[...], xc_ref[...], xn_ref[...]], axis=0)
        hn_ref[...] = (_rms(xe) * g_ref[...]).astype(BF16)

    acc = _dot(hn_ref[...], w_ref[...])
    first = (i % tiles_per_seq) == 0
    last = (i % tiles_per_seq) == tiles_per_seq - 1
    ext = jnp.concatenate([jnp.where(first, 0.0, acc[:SUBLANES]), acc[SUBLANES:SUBLANES + tm],
                           jnp.where(last, 0.0, acc[SUBLANES + tm:])], axis=0)
    n = ext.shape[0]
    nxt = lambda z: pltpu.roll(z, n - 1, 0)
    prv = lambda z: pltpu.roll(z, 1, 0)
    w = [cw_ref[k:k + 1, :] for k in range(SSM_CONV)]
    y = w[2] * ext + nxt(w[3] * ext + nxt(w[4] * ext)) + prv(w[1] * ext + prv(w[0] * ext))
    o_ref[...] = _silu(y[SUBLANES:SUBLANES + tm] + cb_ref[...])


def _ssm_xbc(x, g, w, conv_w, conv_b, S, tm, tn):
    T, D = x.shape
    C = w.shape[1]
    rows8 = tm // SUBLANES
    nblk8 = T // SUBLANES
    return pl.pallas_call(
        functools.partial(_ssm_xbc_kernel, tiles_per_seq=S // tm),
        grid=(T // tm, C // tn),
        in_specs=[
            pl.BlockSpec((SUBLANES, D), lambda i, j: (jnp.maximum(i * rows8 - 1, 0), 0)),
            pl.BlockSpec((tm, D), lambda i, j: (i, 0)),
            pl.BlockSpec((SUBLANES, D), lambda i, j: (jnp.minimum((i + 1) * rows8, nblk8 - 1), 0)),
            pl.BlockSpec((1, D), lambda i, j: (0, 0)),
            pl.BlockSpec((D, tn), lambda i, j: (0, j)),
            pl.BlockSpec((SSM_CONV, tn), lambda i, j: (0, j)),
            pl.BlockSpec((1, tn), lambda i, j: (0, j)),
        ],
        out_specs=pl.BlockSpec((tm, tn), lambda i, j: (i, j)),
        out_shape=jax.ShapeDtypeStruct((T, C), F32),
        scratch_shapes=[pltpu.VMEM((tm + 2 * SUBLANES, D), BF16)],
        compiler_params=_cparams(("parallel", "arbitrary")),
        name="ssm_xbc",
    )(x, x, x, g, w, conv_w, conv_b)


def _ssd_scan_kernel(*refs, reverse, col0, final):
    if final:
        (x_ref, b_ref, c_ref, dt_ref, bias_ref, alog_ref, yo_ref, z_ref, d_ref, nw_ref,
         out_ref, st_ref, y_ref) = refs
    else:
        x_ref, b_ref, c_ref, dt_ref, bias_ref, alog_ref, y_ref, st_ref = refs
    L = SSM_CHUNK
    N = SSM_STATE
    pairs_per_group = SSM_HEADS_PER_GROUP // 2

    @pl.when(pl.program_id(1) == 0)
    def _():
        st_ref[...] = jnp.zeros_like(st_ref)

    log2e = math.log2(math.e)
    dt = _softplus(dt_ref[...] + bias_ref[...])
    a2 = dt * (-jnp.exp(alog_ref[...])) * log2e
    li = lax.broadcasted_iota(jnp.int32, (L, L), 0)
    si = lax.broadcasted_iota(jnp.int32, (L, L), 1)
    tri = (si >= li) if reverse else (si <= li)
    cum = jnp.dot(tri.astype(F32), a2, precision=lax.Precision.HIGHEST, preferred_element_type=F32)
    cum_t = cum.T
    src_t = cum_t - (jnp.log(dt) * log2e).T
    mask = (si > li) if reverse else (si <= li)
    tot = 0 if reverse else L - 1
    low = lax.broadcasted_iota(jnp.int32, (L, LANES), 1) < SSM_HEAD_DIM
    low_row = low[:1]

    c_gs = [c_ref[:, g * N:(g + 1) * N] for g in range(SSM_GROUPS)]
    cbs = [_dot_nt(c_gs[g].astype(BF16), b_ref[:, g * N:(g + 1) * N].astype(BF16))
           for g in range(SSM_GROUPS)]
    b_ts = [b_ref[:, g * N:(g + 1) * N].T for g in range(SSM_GROUPS)]

    for g in range(SSM_GROUPS):
        c_g, cb, b_t = c_gs[g], cbs[g], b_ts[g]
        for jp in range(pairs_per_group):
            pair = g * pairs_per_group + jp
            x16 = x_ref[:, pair * LANES:(pair + 1) * LANES].astype(BF16)
            st = st_ref[pair]
            st16 = st.astype(BF16)
            zero = jnp.zeros_like(x16)
            x_lo, x_hi = jnp.where(low, x16, zero), jnp.where(low, zero, x16)
            s_lo, s_hi = jnp.where(low, st16, zero), jnp.where(low, zero, st16)
            ws, ecs, bts, c_tots = [], [], [], []
            for k in range(2):
                col = col0 + 2 * pair + k
                c_col = jnp.broadcast_to(cum[:, col:col + 1], (L, L))
                src_row = src_t[col:col + 1, :]
                c_tot = cum_t[col:col + 1, tot:tot + 1]
                ws.append(cb * jnp.where(mask, jnp.exp2(c_col - src_row), 0.0))
                ecs.append(c_g * jnp.exp2(c_col))
                bts.append(b_t * jnp.exp2(c_tot - src_row))
                c_tots.append(c_tot)
            lhs = jnp.concatenate(ws + ecs, axis=1).astype(BF16)
            rhs = jnp.concatenate([x_lo, x_hi, s_lo, s_hi], axis=0)
            cols = slice(pair * LANES, (pair + 1) * LANES)
            y = _dot(lhs, rhs)
            if final:
                y = (yo_ref[:, cols] + y + x_ref[:, cols] * d_ref[:, cols]) * _silu(z_ref[:, cols])
                sumsq = y * y if pair == 0 else sumsq + y * y
            y_ref[:, cols] = y
            decay = jnp.where(low_row, jnp.exp2(c_tots[0]), jnp.exp2(c_tots[1]))
            upd = _dot(jnp.concatenate(bts, axis=1).astype(BF16), jnp.concatenate([x_lo, x_hi], axis=0))
            st_ref[pair] = st * decay + upd

    if final:
        scale = lax.rsqrt(jnp.sum(sumsq, axis=1, keepdims=True) * (1.0 / SSM_D_INNER) + EPS)
        out_ref[...] = (y_ref[...] * scale * nw_ref[...]).astype(BF16)


def _ssd_scan(xbc, zdt, dt_bias, a_log, B, S, reverse, tail=None):
    T = B * S
    L = SSM_CHUNK
    nC = S // L
    K = SSM_D_INNER
    col0 = SSM_HEADS if reverse else 0
    dt_blk = K // SSM_DT_PAD
    final = tail is not None

    def row(b, c):
        return b * nC + ((nC - 1 - c) if reverse else c)

    chunk = lambda b, c: (row(b, c), 0)
    const = lambda b, c: (0, 0)
    in_specs = [
        pl.BlockSpec((L, K), chunk),
        pl.BlockSpec((L, SSM_GN), lambda b, c: (row(b, c), K // SSM_GN)),
        pl.BlockSpec((L, SSM_GN), lambda b, c: (row(b, c), K // SSM_GN + 1)),
        pl.BlockSpec((L, SSM_DT_PAD), lambda b, c: (row(b, c), dt_blk)),
        pl.BlockSpec((1, SSM_DT_PAD), const),
        pl.BlockSpec((1, SSM_DT_PAD), const),
    ]
    operands = [xbc, xbc, xbc, zdt, dt_bias, a_log]
    scratch = [pltpu.VMEM((SSM_HEADS // 2, SSM_STATE, 2 * SSM_HEAD_DIM), F32)]
    if final:
        y_other, d_exp, norm_w = tail
        in_specs += [pl.BlockSpec((L, K), chunk), pl.BlockSpec((L, K), chunk),
                     pl.BlockSpec((1, K), const), pl.BlockSpec((1, K), const)]
        operands += [y_other, zdt, d_exp, norm_w]
        scratch.append(pltpu.VMEM((L, K), F32))
    return pl.pallas_call(
        functools.partial(_ssd_scan_kernel, reverse=reverse, col0=col0, final=final),
        grid=(B, nC),
        in_specs=in_specs,
        out_specs=pl.BlockSpec((L, K), chunk),
        out_shape=jax.ShapeDtypeStruct((T, K), BF16 if final else F32),
        scratch_shapes=scratch,
        compiler_params=_cparams(("parallel", "arbitrary")),
        name="ssd_scan_bwd" if reverse else "ssd_scan_fwd",
    )(*operands)


def _ret_qk_kernel(x_ref, g_ref, w_ref, cos_ref, sin_ref, o_ref, xn_ref):
    @pl.when(pl.program_id(1) == 0)
    def _():
        xn_ref[...] = (_rms(x_ref[...]) * g_ref[...]).astype(BF16)

    nf = RET_QK_DIM // 4
    acc = _dot(xn_ref[...], w_ref[...])
    for p in range(o_ref.shape[1] // LANES):
        c0 = p * LANES
        t0 = (p % (RET_QK_DIM // LANES)) * LANES
        h = _rope_piece(acc[:, c0:c0 + LANES], cos_ref[0, :, t0:t0 + LANES],
                        sin_ref[0, :, t0:t0 + LANES], nf)
        o_ref[:, c0:c0 + LANES] = h.astype(o_ref.dtype)


def _ret_qk(x, g, w, cos, sin, S, tm):
    T, D = x.shape
    nS = S // tm
    return pl.pallas_call(
        _ret_qk_kernel,
        grid=(T // tm, 2),
        in_specs=[
            pl.BlockSpec((tm, D), lambda i, j: (i, 0)),
            pl.BlockSpec((1, D), lambda i, j: (0, 0)),
            pl.BlockSpec((D, D), lambda i, j: (0, j)),
            pl.BlockSpec((1, tm, RET_QK_DIM), lambda i, j: (j, i % nS, 0)),
            pl.BlockSpec((1, tm, RET_QK_DIM), lambda i, j: (j, i % nS, 0)),
        ],
        out_specs=pl.BlockSpec((tm, D), lambda i, j: (i, j)),
        out_shape=jax.ShapeDtypeStruct((T, 2 * D), BF16),
        scratch_shapes=[pltpu.VMEM((tm, D), BF16)],
        compiler_params=_cparams(("parallel", "arbitrary")),
        name="ret_qk",
    )(x, g, w, cos, sin)


def _ret_scan_kernel(*refs, reverse, row0, final):
    if final:
        qk_ref, v_ref, p_ref, yo_ref, gate_ref, out_ref, r_ref, y_ref = refs
    else:
        qk_ref, v_ref, p_ref, y_ref, r_ref = refs
    L = RET_CHUNK
    dk, dv = RET_QK_DIM, RET_V_DIM
    rep = dv // LANES

    @pl.when(pl.program_id(1) == 0)
    def _():
        r_ref[...] = jnp.zeros_like(r_ref)

    li = lax.broadcasted_iota(jnp.int32, (L, LANES), 0)
    si = lax.broadcasted_iota(jnp.int32, (L, LANES), 1)
    lf = li.astype(F32)
    if reverse:
        mask = si > li
        diff = (si - li).astype(F32)
        q_pow = L - lf
        k_pow = lf
    else:
        mask = si <= li
        diff = (li - si).astype(F32)
        q_pow = lf + 1.0
        k_pow = L - 1.0 - lf

    heads = range(RET_HEADS)
    qs = [qk_ref[:, h * dk:(h + 1) * dk] for h in heads]
    ks = [qk_ref[:, RET_HEADS * dk + h * dk:RET_HEADS * dk + (h + 1) * dk] for h in heads]
    lgs = [jnp.log1p(-jnp.exp2(p_ref[row0 + h:row0 + h + 1, :])) for h in heads]
    qk = [_dot_nt(qs[h], ks[h]) for h in heads]
    for h in heads:
        lg = lgs[h]
        q_dec = jnp.exp(q_pow * lg)
        k_dec = jnp.exp(k_pow * lg)
        v = v_ref[:, h * dv:(h + 1) * dv]
        r = r_ref[h]
        y_cross = _dot(qs[h], r.astype(BF16)) * jnp.concatenate([q_dec] * rep, axis=1)
        y_ref[:, h * dv:(h + 1) * dv] = y_cross
        v_sc = (v.astype(F32) * jnp.concatenate([k_dec] * rep, axis=1)).astype(BF16)
        r_ref[h] = r * jnp.exp(float(L) * lg[:, :1]) + _dot_tn(ks[h], v_sc)
    for h in heads:
        dmat = jnp.where(mask, jnp.exp(diff * lgs[h]), 0.0)
        scores = (qk[h] * dmat).astype(BF16)
        cols = slice(h * dv, (h + 1) * dv)
        y = y_ref[:, cols] + _dot(scores, v_ref[:, cols])
        if final:
            out_ref[:, cols] = (_silu(gate_ref[:, cols]) * _rms(yo_ref[:, cols] + y)).astype(BF16)
        else:
            y_ref[:, cols] = y


def _ret_scan(qk, v, p_rows, B, S, reverse, tail=None):
    T = B * S
    L = RET_CHUNK
    nC = S // L
    K = RET_HEADS * RET_V_DIM
    row0 = RET_HEADS if reverse else 0
    final = tail is not None

    def row(b, c):
        return b * nC + ((nC - 1 - c) if reverse else c)

    chunk = lambda b, c: (row(b, c), 0)
    in_specs = [
        pl.BlockSpec((L, 2 * D_MODEL), chunk),
        pl.BlockSpec((L, K), chunk),
        pl.BlockSpec((2 * RET_HEADS, LANES), lambda b, c: (0, 0)),
    ]
    operands = [qk, v, p_rows]
    scratch = [pltpu.VMEM((RET_HEADS, RET_QK_DIM, RET_V_DIM), F32)]
    if final:
        in_specs += [pl.BlockSpec((L, K), chunk), pl.BlockSpec((L, K), chunk)]
        operands += list(tail)
        scratch.append(pltpu.VMEM((L, K), F32))
    return pl.pallas_call(
        functools.partial(_ret_scan_kernel, reverse=reverse, row0=row0, final=final),
        grid=(B, nC),
        in_specs=in_specs,
        out_specs=pl.BlockSpec((L, K), chunk),
        out_shape=jax.ShapeDtypeStruct((T, K), BF16 if final else F32),
        scratch_shapes=scratch,
        compiler_params=_cparams(("parallel", "arbitrary")),
        name="ret_scan_bwd" if reverse else "ret_scan_fwd",
    )(*operands)


def _rope_tables(S, dim, scale):
    rows = S // GRID_W
    half = dim // 2
    nf = half // 2
    inv = ROPE_THETA ** (-(jnp.arange(nf, dtype=F32) * 2.0) / half)
    ar = jnp.arange(rows, dtype=F32)[:, None] * inv
    ac = jnp.arange(GRID_W, dtype=F32)[:, None] * inv
    by_row = lambda t: jnp.repeat(t, GRID_W, axis=0)
    by_col = lambda t: jnp.tile(t, (rows, 1))
    cr, sr = by_row(jnp.cos(ar) * scale), by_row(jnp.sin(ar) * scale)
    cc, sc = by_col(jnp.cos(ac) * scale), by_col(jnp.sin(ac) * scale)
    return jnp.concatenate([cr, cr, cc, cc], axis=1), jnp.concatenate([-sr, sr, -sc, sc], axis=1)


def _row(v):
    return v.reshape(1, -1).astype(F32)


def _tile(n, pref):
    return pref if n % pref == 0 else n


def _ssm_layer(x, B, S, g_pre, g_post, w_in, conv_w, conv_b, dt_bias, a_log, d_skip, norm_w, w_out):
    T = B * S
    w = w_in.astype(BF16)
    g = _row(g_pre)
    xbc_end = SSM_D_INNER + SSM_CONV_DIM
    w_zdt = jnp.concatenate([w[:, :SSM_D_INNER], w[:, xbc_end:]], axis=1)
    w_zdt = jnp.pad(w_zdt, ((0, 0), (0, SSM_ZDT_COLS - w_zdt.shape[1])))
    zdt = _norm_proj(x, g, w_zdt, _tile(T, 512), SSM_ZDT_COLS, F32, "ssm_zdt")
    xbc = _ssm_xbc(x, g, w[:, SSM_D_INNER:xbc_end], conv_w.astype(F32), _row(conv_b), S,
                   _tile(S, 1024), 1024)
    lane_pad = SSM_DT_PAD - 2 * SSM_HEADS
    bias = jnp.pad(dt_bias.reshape(1, -1).astype(F32), ((0, 0), (0, lane_pad)))
    alog = jnp.pad(a_log.reshape(1, -1).astype(F32), ((0, 0), (0, lane_pad)))
    d_exp = jnp.repeat(d_skip.astype(F32), SSM_HEAD_DIM).reshape(1, -1)
    yf = _ssd_scan(xbc, zdt, bias, alog, B, S, reverse=False)
    y = _ssd_scan(xbc, zdt, bias, alog, B, S, reverse=True, tail=(yf, d_exp, _row(norm_w)))
    return _out_proj(y, w_out.astype(BF16), _row(g_post), x, _tile(T, 1024), "ssm_out")


def _att_layer(x, B, S, g_pre, g_post, w_in, q_norm, k_norm, w_out):
    T = B * S
    hd = ATT_HEAD_DIM
    nq, nkv = ATT_Q_HEADS * hd, ATT_KV_HEADS * hd
    cq, sq = _rope_tables(S, hd, hd ** -0.5 * math.log2(math.e))
    ck, sk = _rope_tables(S, hd, 1.0)
    w = w_in.astype(BF16)
    qn_b = jnp.broadcast_to(q_norm.astype(F32).reshape(hd, 1), (hd, LANES))
    qt, k, vt = _att_inproj(x, _row(g_pre), w[:, :nq].T, w[:, nq:nq + nkv], w[:, nq + nkv:].T,
                            qn_b, _row(k_norm), cq.T, sq.T, ck, sk, S, _tile(S, 512))
    o = _flash(qt, k, vt, B, S, _tile(S, 512), _tile(S, 2048))
    return _out_proj(o, w_out.astype(BF16), _row(g_post), x, _tile(T, 1024), "att_out")


def _ret_layer(x, B, S, g_pre, g_post, w_in, log2_decay, w_out):
    T = B * S
    D = D_MODEL
    cq, sq = _rope_tables(S, RET_QK_DIM, 1.0)
    ck, sk = _rope_tables(S, RET_QK_DIM, RET_QK_DIM ** -0.5)
    cos = jnp.stack([cq, ck])
    sin = jnp.stack([sq, sk])
    w = w_in.astype(BF16)
    g = _row(g_pre)
    qk = _ret_qk(x, g, w[:, :2 * D], cos, sin, S, _tile(S, 1024))
    v = _norm_proj(x, g, w[:, 2 * D:4 * D], _tile(T, 1024), 1024, BF16, "ret_v")
    gate = _norm_proj(x, g, w[:, 4 * D:], _tile(T, 1024), 1024, F32, "ret_gate")
    p_rows = jnp.broadcast_to(log2_decay.reshape(-1, 1).astype(F32), (2 * RET_HEADS, LANES))
    yf = _ret_scan(qk, v, p_rows, B, S, reverse=False)
    y = _ret_scan(qk, v, p_rows, B, S, reverse=True, tail=(yf, gate))
    return _out_proj(y, w_out.astype(BF16), _row(g_post), x, _tile(T, 1024), "ret_out")


def kernel(x, norm_mix_pre, norm_mix_post, norm_ffn_pre, norm_ffn_post, mlp_w_in, mlp_w_out,
           ssm_w_in, ssm_conv_w, ssm_conv_b, ssm_dt_bias, ssm_a_log, ssm_d, ssm_norm, ssm_w_out,
           att_w_in, att_q_norm, att_k_norm, att_w_out,
           ret_w_in, ret_log2_decay, ret_w_out):
    B, S, D = x.shape
    T = B * S
    depth = norm_mix_pre.shape[0]
    h = x.reshape(T, D)
    mlp_w1 = mlp_w_in.astype(BF16)
    mlp_w2 = mlp_w_out.astype(BF16)
    for i in range(depth):
        kind, j = i % 3, i // 3
        if kind == 0:
            h = _ssm_layer(h, B, S, norm_mix_pre[i], norm_mix_post[i], ssm_w_in[j], ssm_conv_w[j],
                           ssm_conv_b[j], ssm_dt_bias[j], ssm_a_log[j], ssm_d[j], ssm_norm[j],
                           ssm_w_out[j])
        elif kind == 1:
            h = _att_layer(h, B, S, norm_mix_pre[i], norm_mix_post[i], att_w_in[j], att_q_norm[j],
                           att_k_norm[j], att_w_out[j])
        else:
            h = _ret_layer(h, B, S, norm_mix_pre[i], norm_mix_post[i], ret_w_in[j],
                           ret_log2_decay[j], ret_w_out[j])
        h = _mlp(h, _row(norm_ffn_pre[i]), mlp_w1, mlp_w2, _row(norm_ffn_post[i]), i,
                 _tile(T, 1024), 1024)
    return h.reshape(B, S, D)
```
